```python
import jax, jax.numpy as jnp
from jax import lax
import numpy as np

D_MODEL = 1024
BATCH = 16
SEQ = 2048
DEPTH = 1

D_FF = 2816
MLSTM_HEADS = 4
MLSTM_DQK = 128
MLSTM_DV = 256
MLSTM_CHUNK = 64
CONV_WIDTH = 4
FOX_HEADS = 16
FOX_DH = 64
FOX_BLOCK = 128
N_MOD = 9
EPS = 1e-6
MLSTM_QK = MLSTM_HEADS * MLSTM_DQK
MLSTM_V = MLSTM_HEADS * MLSTM_DV
FOX_W = FOX_HEADS * FOX_DH
MIX_SPLITS = (MLSTM_QK, MLSTM_QK, MLSTM_V, MLSTM_V, MLSTM_HEADS, MLSTM_HEADS, FOX_W, FOX_W, FOX_W, FOX_HEADS, D_MODEL, D_MODEL)
MIX_WIDTH = 2 * MLSTM_QK + 2 * MLSTM_V + 2 * MLSTM_HEADS + 3 * FOX_W + FOX_HEADS + 2 * D_MODEL

kernel_name = "hybrid_mlstm_fox_macaron_adaln"


def _mix_offsets():
    offs = [0]
    for s in MIX_SPLITS:
        offs.append(offs[-1] + s)
    return offs


def rms_norm(x, g):
    xf = x.astype(jnp.float32)
    y = xf * lax.rsqrt(jnp.mean(xf * xf, axis=-1, keepdims=True) + EPS)
    return (y * g.astype(jnp.float32)).astype(x.dtype)


def swiglu(u, w_in, w_out):
    a, b = jnp.split(u @ w_in, 2, axis=-1)
    return (jax.nn.silu(a) * b) @ w_out


def causal_conv(x, w, b):
    y = lax.conv_general_dilated(x, w[:, None, :], window_strides=(1,), padding=[(CONV_WIDTH - 1, 0)],
                                 dimension_numbers=('NWC', 'WIO', 'NWC'), feature_group_count=x.shape[-1])
    return y + b


def mlstm_chunkwise(q, k, v, i_pre, f_pre):
    B, S, H, _ = q.shape
    L = MLSTM_CHUNK
    nc = S // L
    f32 = jnp.float32

    def to_chunks(t):
        t = jnp.moveaxis(t.astype(f32), 2, 1)
        t = t.reshape((B, H, nc, L) + t.shape[3:])
        return jnp.moveaxis(t, 2, 0)

    qc = to_chunks(q)
    kc = to_chunks(k) * (MLSTM_DQK ** -0.5)
    vc = to_chunks(v)
    ic = to_chunks(i_pre)
    lfc = to_chunks(jax.nn.log_sigmoid(f_pre.astype(f32)))
    causal = jnp.tril(jnp.ones((L, L), dtype=bool))

    def step(carry, inp):
        C, n, m = carry
        qt, kt, vt, it, lft = inp
        b = jnp.cumsum(lft, axis=-1)
        log_d = jnp.where(causal, b[..., :, None] - b[..., None, :] + it[..., None, :], -jnp.inf)
        log_inter = b + m[..., None]
        m_t = jnp.maximum(log_inter, jnp.max(log_d, axis=-1))
        w_inter = jnp.exp(log_inter - m_t)
        s = jnp.einsum('bhtd,bhsd->bhts', qt, kt) * jnp.exp(log_d - m_t[..., None])
        num = w_inter[..., None] * jnp.einsum('bhtd,bhde->bhte', qt, C) + jnp.einsum('bhts,bhse->bhte', s, vt)
        den = w_inter * jnp.einsum('bhtd,bhd->bht', qt, n) + jnp.sum(s, axis=-1)
        h = num / jnp.maximum(jnp.abs(den), jnp.exp(-m_t))[..., None]
        b_last = b[..., -1]
        log_w = b_last[..., None] - b + it
        m_new = jnp.maximum(b_last + m, jnp.max(log_w, axis=-1))
        w_k = jnp.exp(log_w - m_new[..., None])
        decay = jnp.exp(b_last + m - m_new)
        C_new = decay[..., None, None] * C + jnp.einsum('bhs,bhsd,bhse->bhde', w_k, kt, vt)
        n_new = decay[..., None] * n + jnp.einsum('bhs,bhsd->bhd', w_k, kt)
        return (C_new, n_new, m_new), h

    init = (jnp.zeros((B, H, MLSTM_DQK, MLSTM_DV), f32), jnp.zeros((B, H, MLSTM_DQK), f32), jnp.zeros((B, H), f32))
    _, h = lax.scan(step, init, (qc, kc, vc, ic, lfc))
    h = jnp.moveaxis(h, 0, 2).reshape(B, H, S, MLSTM_DV)
    return jnp.moveaxis(h, 1, 2)


def forgetting_attention(q, k, v, f_pre, q_g, k_g):
    B, S, H, Dh = q.shape
    f32 = jnp.float32
    q = jnp.moveaxis(rms_norm(q, q_g), 1, 2)
    k = jnp.moveaxis(rms_norm(k, k_g), 1, 2)
    v = jnp.moveaxis(v, 1, 2)
    F = jnp.moveaxis(jnp.cumsum(jax.nn.log_sigmoid(f_pre.astype(f32)), axis=1), 1, 2)
    nb = S // FOX_BLOCK
    qb = jnp.moveaxis(q.reshape(B, H, nb, FOX_BLOCK, Dh), 2, 0)
    Fb = jnp.moveaxis(F.reshape(B, H, nb, FOX_BLOCK), 2, 0)
    k_pos = jnp.arange(S)
    scale = Dh ** -0.5

    def block(args):
        q_blk, F_blk, blk_idx = args
        q_pos = blk_idx * FOX_BLOCK + jnp.arange(FOX_BLOCK)
        logits = jnp.einsum('bhqd,bhkd->bhqk', q_blk, k).astype(f32) * scale + (F_blk[..., :, None] - F[..., None, :])
        logits = jnp.where(k_pos[None, :] <= q_pos[:, None], logits, -jnp.inf)
        p = jax.nn.softmax(logits, axis=-1)
        return jnp.einsum('bhqk,bhkd->bhqd', p.astype(v.dtype), v)

    out = lax.map(block, (qb, Fb, jnp.arange(nb)))
    out = jnp.moveaxis(out, 0, 2).reshape(B, H, S, Dh)
    return jnp.moveaxis(out, 1, 2).reshape(B, S, H * Dh)


def hybrid_layer(x, c, w_ada, b_ada, ffn1_norm_g, ffn1_w_in, ffn1_w_out, mix_norm_g, w_mix, b_mix,
                 conv_w, conv_b, mlstm_norm_g, fox_q_norm_g, fox_k_norm_g, w_branch_a, w_branch_b, w_out,
                 ffn2_norm_g, ffn2_w_in, ffn2_w_out):
    B, S, _ = x.shape
    mod = jax.nn.silu(c) @ w_ada + b_ada
    sh1, sc1, g1, sh2, sc2, g2, sh3, sc3, g3 = [m[:, None, :] for m in jnp.split(mod, N_MOD, axis=-1)]

    u = rms_norm(x, ffn1_norm_g) * (1 + sc1) + sh1
    x = x + 0.5 * g1 * swiglu(u, ffn1_w_in, ffn1_w_out)

    u = rms_norm(x, mix_norm_g) * (1 + sc2) + sh2
    z = u @ w_mix + b_mix
    offs = _mix_offsets()
    q_m, k_m, v_m, o_m, i_m, f_m, q_f, k_f, v_f, f_f, g_a, g_b = [z[..., offs[j]:offs[j + 1]] for j in range(len(MIX_SPLITS))]

    qk_m = jax.nn.silu(causal_conv(jnp.concatenate([q_m, k_m], axis=-1), conv_w, conv_b))
    q_m, k_m = jnp.split(qk_m, 2, axis=-1)
    h_m = mlstm_chunkwise(q_m.reshape(B, S, MLSTM_HEADS, MLSTM_DQK), k_m.reshape(B, S, MLSTM_HEADS, MLSTM_DQK),
                          v_m.reshape(B, S, MLSTM_HEADS, MLSTM_DV), i_m, f_m)
    y_a = jax.nn.sigmoid(o_m) * rms_norm(h_m, mlstm_norm_g).reshape(B, S, MLSTM_V).astype(x.dtype)

    y_b = forgetting_attention(q_f.reshape(B, S, FOX_HEADS, FOX_DH), k_f.reshape(B, S, FOX_HEADS, FOX_DH),
                               v_f.reshape(B, S, FOX_HEADS, FOX_DH), f_f, fox_q_norm_g, fox_k_norm_g)

    merged = jax.nn.sigmoid(g_a) * (y_a @ w_branch_a) + jax.nn.sigmoid(g_b) * (y_b @ w_branch_b)
    x = x + g2 * (merged @ w_out)

    u = rms_norm(x, ffn2_norm_g) * (1 + sc3) + sh3
    x = x + 0.5 * g3 * swiglu(u, ffn2_w_in, ffn2_w_out)
    return x


def setup_inputs(seed: int = 0) -> dict:
    key = jax.random.key(seed)
    ks = jax.random.split(key, 24)
    f32 = jnp.float32
    L = DEPTH
    D = D_MODEL

    def nrm(k, shape, scale):
        return jax.random.normal(k, shape, f32) * scale

    def gain(k, shape):
        return 1.0 + 0.05 * jax.random.normal(k, shape, f32)

    offs = _mix_offsets()
    b_mix = nrm(ks[9], (L, MIX_WIDTH), 0.02)
    b_mix = b_mix.at[:, offs[5]:offs[6]].add(jnp.linspace(3.0, 6.0, MLSTM_HEADS))
    b_mix = b_mix.at[:, offs[9]:offs[10]].add(jnp.linspace(2.0, 7.0, FOX_HEADS))
    return {
        "x": nrm(ks[0], (BATCH, SEQ, D), 1.0),
        "c": nrm(ks[1], (BATCH, D), 1.0),
        "w_ada": nrm(ks[2], (L, D, N_MOD * D), 0.5 * D ** -0.5),
        "b_ada": nrm(ks[3], (L, N_MOD * D), 0.02),
        "ffn1_norm_g": gain(ks[4], (L, D)),
        "ffn1_w_in": nrm(ks[5], (L, D, 2 * D_FF), D ** -0.5),
        "ffn1_w_out": nrm(ks[6], (L, D_FF, D), D_FF ** -0.5),
        "mix_norm_g": gain(ks[7], (L, D)),
        "w_mix": nrm(ks[8], (L, D, MIX_WIDTH), D ** -0.5),
        "b_mix": b_mix,
        "conv_w": nrm(ks[10], (L, CONV_WIDTH, 2 * MLSTM_QK), CONV_WIDTH ** -0.5),
        "conv_b": nrm(ks[11], (L, 2 * MLSTM_QK), 0.02),
        "mlstm_norm_g": gain(ks[12], (L, MLSTM_HEADS, MLSTM_DV)),
        "fox_q_norm_g": gain(ks[13], (L, FOX_HEADS, FOX_DH)),
        "fox_k_norm_g": gain(ks[14], (L, FOX_HEADS, FOX_DH)),
        "w_branch_a": nrm(ks[15], (L, MLSTM_V, D), MLSTM_V ** -0.5),
        "w_branch_b": nrm(ks[16], (L, FOX_W, D), FOX_W ** -0.5),
        "w_out": nrm(ks[17], (L, D, D), D ** -0.5),
        "ffn2_norm_g": gain(ks[18], (L, D)),
        "ffn2_w_in": nrm(ks[19], (L, D, 2 * D_FF), D ** -0.5),
        "ffn2_w_out": nrm(ks[20], (L, D_FF, D), D_FF ** -0.5),
    }


def reference(x, c, w_ada, b_ada, ffn1_norm_g, ffn1_w_in, ffn1_w_out, mix_norm_g, w_mix, b_mix,
              conv_w, conv_b, mlstm_norm_g, fox_q_norm_g, fox_k_norm_g, w_branch_a, w_branch_b, w_out,
              ffn2_norm_g, ffn2_w_in, ffn2_w_out):
    for l in range(DEPTH):
        x = hybrid_layer(x, c, w_ada[l], b_ada[l], ffn1_norm_g[l], ffn1_w_in[l], ffn1_w_out[l], mix_norm_g[l],
                         w_mix[l], b_mix[l], conv_w[l], conv_b[l], mlstm_norm_g[l], fox_q_norm_g[l],
                         fox_k_norm_g[l], w_branch_a[l], w_branch_b[l], w_out[l], ffn2_norm_g[l],
                         ffn2_w_in[l], ffn2_w_out[l])
    return x
```

```python
import functools

import jax
import jax.numpy as jnp
from jax import lax
from jax.experimental import pallas as pl
from jax.experimental.pallas import tpu as pltpu

F32 = jnp.float32
BF16 = jnp.bfloat16
HIGHEST = lax.Precision.HIGHEST

D_MODEL = 1024
D_FF = 2816
MLSTM_HEADS = 4
MLSTM_DQK = 128
MLSTM_DV = 256
MLSTM_CHUNK = 64
CONV_WIDTH = 4
FOX_HEADS = 16
FOX_DH = 64
N_MOD = 9
EPS = 1e-6
MLSTM_QK = MLSTM_HEADS * MLSTM_DQK
MLSTM_V = MLSTM_HEADS * MLSTM_DV
FOX_W = FOX_HEADS * FOX_DH
MIX_SPLITS = (MLSTM_QK, MLSTM_QK, MLSTM_V, MLSTM_V, MLSTM_HEADS, MLSTM_HEADS, FOX_W, FOX_W, FOX_W,
              FOX_HEADS, D_MODEL, D_MODEL)

LANES = 128
SUBLANES = 8
FF_CHUNK = 256
N_GATES = 2 * MLSTM_HEADS + FOX_HEADS
VMEM_LIMIT = 56 * 1024 * 1024


def _mix_offsets():
    offs = [0]
    for s in MIX_SPLITS:
        offs.append(offs[-1] + s)
    return offs


def _resident(shape):
    nd = len(shape)
    return pl.BlockSpec(shape, lambda *_: (0,) * nd, pipeline_mode=pl.Buffered(1))


def _params(sem):
    return pltpu.CompilerParams(dimension_semantics=sem, vmem_limit_bytes=VMEM_LIMIT)


def _sigmoid(x):
    return 1.0 / (1.0 + jnp.exp(-x))


def _adaln_kernel(c_ref, w_ref, b_ref, o_ref):
    c = c_ref[...]
    s = c * _sigmoid(c)
    o_ref[...] = jnp.dot(s, w_ref[...], preferred_element_type=F32, precision=HIGHEST) + b_ref[...]


def _adaln(c, w, b):
    bsz, d = c.shape
    n = w.shape[1]
    tn = 1024
    return pl.pallas_call(
        _adaln_kernel,
        grid=(n // tn,),
        in_specs=[pl.BlockSpec((bsz, d), lambda j: (0, 0)),
                  pl.BlockSpec((d, tn), lambda j: (0, j)),
                  pl.BlockSpec((1, tn), lambda j: (0, j))],
        out_specs=pl.BlockSpec((bsz, tn), lambda j: (0, j)),
        out_shape=jax.ShapeDtypeStruct((bsz, n), F32),
        compiler_params=_params(("arbitrary",)),
        name="adaln",
    )(c, w, b.reshape(1, n))


def _modulated_norm(x, g, sc, sh):
    ms = jnp.mean(x * x, axis=-1, keepdims=True)
    return (x * lax.rsqrt(ms + EPS) * g) * (1.0 + sc) + sh


def _ffn_kernel(x_ref, sh_ref, sc_ref, gt_ref, g_ref, win_ref, wout_ref, o_ref, acc_ref):
    x = x_ref[...]
    ub = _modulated_norm(x, g_ref[...], sc_ref[...], sh_ref[...]).astype(BF16)
    acc_ref[...] = jnp.zeros_like(acc_ref)

    def body(j, carry):
        ab = jnp.dot(ub, win_ref[j], preferred_element_type=F32)
        a = ab[:, :FF_CHUNK]
        b = ab[:, FF_CHUNK:]
        h = (a * _sigmoid(a) * b).astype(BF16)
        acc_ref[...] += jnp.dot(h, wout_ref[j], preferred_element_type=F32)
        return carry

    lax.fori_loop(0, win_ref.shape[0], body, 0)
    o_ref[...] = x + (0.5 * gt_ref[...]) * acc_ref[...]


def _ffn(x, mod4, mod_idx, g, win3, wout3, tm=512):
    bsz, seq, d = x.shape
    nt = seq // tm
    row = lambda k: pl.BlockSpec((None, None, 1, d), lambda b, i: (b, k, 0, 0))
    return pl.pallas_call(
        _ffn_kernel,
        grid=(bsz, nt),
        in_specs=[pl.BlockSpec((None, tm, d), lambda b, i: (b, i, 0)),
                  row(mod_idx), row(mod_idx + 1), row(mod_idx + 2),
                  _resident((1, d)), _resident(win3.shape), _resident(wout3.shape)],
        out_specs=pl.BlockSpec((None, tm, d), lambda b, i: (b, i, 0)),
        out_shape=jax.ShapeDtypeStruct(x.shape, F32),
        scratch_shapes=[pltpu.VMEM((tm, d), F32)],
        compiler_params=_params(("arbitrary", "arbitrary")),
        name="ffn",
    )(x, mod4, mod4, mod4, g.reshape(1, d), win3, wout3)


def _mix_kernel(x_ref, sh_ref, sc_ref, g_ref, wqk_ref, wv_ref, wo_ref, wfox_ref, wg_ref, wgh_ref, wgl_ref,
                bqk_ref, bv_ref, bo_ref, bfox_ref, bg_ref, bgate_ref, cw_ref, cb_ref,
                qk_out, v_out, o_out, fox_out, g_out, gate_out, halo_ref):
    i = pl.program_id(1)
    tm = x_ref.shape[0]

    @pl.when(i == 0)
    def _():
        halo_ref[...] = jnp.zeros_like(halo_ref)

    u = _modulated_norm(x_ref[...], g_ref[...], sc_ref[...], sh_ref[...])
    ub = u.astype(BF16)
    ul = (u - ub.astype(F32)).astype(BF16)

    def proj(w_ref, b_ref):
        return jnp.dot(ub, w_ref[...], preferred_element_type=F32) + b_ref[...]

    v_out[...] = proj(wv_ref, bv_ref).astype(BF16)
    o_out[...] = proj(wo_ref, bo_ref).astype(BF16)
    fox_out[...] = proj(wfox_ref, bfox_ref).astype(BF16)
    g_out[...] = proj(wg_ref, bg_ref).astype(BF16)
    gate_out[...] = (jnp.dot(ub, wgh_ref[...], preferred_element_type=F32)
                     + jnp.dot(ul, wgh_ref[...], preferred_element_type=F32)
                     + jnp.dot(ub, wgl_ref[...], preferred_element_type=F32) + bgate_ref[...])

    z = proj(wqk_ref, bqk_ref)
    halo = halo_ref[...]
    rows = lax.broadcasted_iota(jnp.int32, (tm, 1), 0)
    y = z * cw_ref[CONV_WIDTH - 1:CONV_WIDTH, :] + cb_ref[...]
    for k in range(1, CONV_WIDTH):
        zk = pltpu.roll(z, k, 0)
        hk = pltpu.roll(halo, k, 0)
        hk_full = jnp.concatenate([hk, zk[SUBLANES:, :]], axis=0)
        zk = jnp.where(rows < k, hk_full, zk)
        y = y + zk * cw_ref[CONV_WIDTH - 1 - k:CONV_WIDTH - k, :]
    halo_ref[...] = z[tm - SUBLANES:, :]
    y = y * _sigmoid(y)
    lane = lax.broadcasted_iota(jnp.int32, (1, 2 * MLSTM_QK), 1)
    y = y * jnp.where(lane < MLSTM_QK, 1.0, MLSTM_DQK ** -0.5)
    qk_out[...] = y.astype(BF16)


def _mix(x, mod4, g, w, tm=256):
    bsz, seq, d = x.shape
    nt = seq // tm
    row = lambda k: pl.BlockSpec((None, None, 1, d), lambda b, i: (b, k, 0, 0))
    tile = lambda n: pl.BlockSpec((None, tm, n), lambda b, i: (b, i, 0))
    names = ("wqk", "wv", "wo", "wfox", "wg", "wgh", "wgl", "bqk", "bv", "bo", "bfox", "bg", "bgate", "cw", "cb")
    widths = (2 * MLSTM_QK, MLSTM_V, MLSTM_V, 3 * FOX_W, 2 * D_MODEL, LANES)
    dtypes = (BF16, BF16, BF16, BF16, BF16, F32)
    return pl.pallas_call(
        _mix_kernel,
        grid=(bsz, nt),
        in_specs=[tile(d), row(3), row(4), _resident((1, d))] + [_resident(w[n].shape) for n in names],
        out_specs=[tile(n) for n in widths],
        out_shape=[jax.ShapeDtypeStruct((bsz, seq, n), dt) for n, dt in zip(widths, dtypes)],
        scratch_shapes=[pltpu.VMEM((SUBLANES, 2 * MLSTM_QK), F32)],
        compiler_params=_params(("arbitrary", "arbitrary")),
        name="mix",
    )(x, mod4, mod4, g.reshape(1, d), *[w[n] for n in names])


def _gates_kernel(z_ref, o_ref, carry_ref):
    i = pl.program_id(1)
    tm = z_ref.shape[0]

    @pl.when(i == 0)
    def _():
        carry_ref[...] = jnp.zeros_like(carry_ref)

    z = z_ref[...]
    lf = jnp.minimum(z, 0.0) - jnp.log1p(jnp.exp(-jnp.abs(z)))
    row = lax.broadcasted_iota(jnp.int32, (tm, tm), 0)
    col = lax.broadcasted_iota(jnp.int32, (tm, tm), 1)
    tril = col <= row
    shift = MLSTM_CHUNK.bit_length() - 1
    same_chunk = (row >> shift) == (col >> shift)
    cum_full = jnp.dot(tril.astype(F32), lf, preferred_element_type=F32, precision=HIGHEST) + carry_ref[0:1, :]
    cum_chunk = jnp.dot((tril & same_chunk).astype(F32), lf, preferred_element_type=F32, precision=HIGHEST)
    lane = lax.broadcasted_iota(jnp.int32, (tm, LANES), 1)
    o_ref[...] = jnp.where(lane < MLSTM_HEADS, z, jnp.where(lane < 2 * MLSTM_HEADS, cum_chunk, cum_full))
    carry_ref[...] = jnp.broadcast_to(cum_full[tm - 1:tm, :], carry_ref.shape)


def _gates(z, tm=512):
    bsz, seq, n = z.shape
    return pl.pallas_call(
        _gates_kernel,
        grid=(bsz, seq // tm),
        in_specs=[pl.BlockSpec((None, tm, n), lambda b, i: (b, i, 0))],
        out_specs=pl.BlockSpec((None, tm, n), lambda b, i: (b, i, 0)),
        out_shape=jax.ShapeDtypeStruct(z.shape, F32),
        scratch_shapes=[pltpu.VMEM((SUBLANES, n), F32)],
        compiler_params=_params(("arbitrary", "arbitrary")),
        name="gates",
    )(z)


def _mlstm_kernel(qk_ref, v_ref, o_ref, gcol_ref, grow_ref, ng_ref, y_ref, c_ref, n_ref, m_ref):
    ci = pl.program_id(1)
    L = MLSTM_CHUNK

    @pl.when(ci == 0)
    def _():
        c_ref[...] = jnp.zeros_like(c_ref)
        n_ref[...] = jnp.zeros_like(n_ref)
        m_ref[...] = jnp.zeros_like(m_ref)

    row = lax.broadcasted_iota(jnp.int32, (L, L), 0)
    col = lax.broadcasted_iota(jnp.int32, (L, L), 1)
    causal = col <= row
    gcol = gcol_ref[...]
    grow = grow_ref[...]
    nt = (((1,), (1,)), ((), ()))
    tn = (((0,), (0,)), ((), ()))
    for h in range(MLSTM_HEADS):
        q = qk_ref[:, h * MLSTM_DQK:(h + 1) * MLSTM_DQK]
        k = qk_ref[:, MLSTM_QK + h * MLSTM_DQK:MLSTM_QK + (h + 1) * MLSTM_DQK]
        v = v_ref[:, h * MLSTM_DV:(h + 1) * MLSTM_DV]
        i_c = gcol[:, h:h + 1]
        b_c = gcol[:, MLSTM_HEADS + h:MLSTM_HEADS + h + 1]
        i_r = grow[h:h + 1, :]
        b_r = grow[MLSTM_HEADS + h:MLSTM_HEADS + h + 1, :]
        m = m_ref[h][0:1, 0:1]
        cmat = c_ref[h]
        n = n_ref[h][0:1, :]

        log_d = jnp.where(causal, b_c - b_r + i_r, -jnp.inf)
        log_inter = b_c + m
        m_t = jnp.maximum(log_inter, jnp.max(log_d, axis=1, keepdims=True))
        w_inter = jnp.exp(log_inter - m_t)
        s = lax.dot_general(q, k, nt, preferred_element_type=F32) * jnp.exp(log_d - m_t)
        num = (w_inter * jnp.dot(q, cmat.astype(BF16), preferred_element_type=F32)
               + jnp.dot(s.astype(BF16), v, preferred_element_type=F32))
        den = (w_inter * jnp.sum(q.astype(F32) * n, axis=1, keepdims=True)
               + jnp.sum(s, axis=1, keepdims=True))
        hh = num / jnp.maximum(jnp.abs(den), jnp.exp(-m_t))
        ms = jnp.mean(hh * hh, axis=1, keepdims=True)
        hn = hh * lax.rsqrt(ms + EPS) * ng_ref[h:h + 1, :]
        og = o_ref[:, h * MLSTM_DV:(h + 1) * MLSTM_DV].astype(F32)
        y_ref[:, h * MLSTM_DV:(h + 1) * MLSTM_DV] = (_sigmoid(og) * hn).astype(BF16)

        b_last = b_r[:, L - 1:L]
        log_w = b_last - b_c + i_c
        m_new = jnp.maximum(b_last + m, jnp.max(log_w, axis=0, keepdims=True))
        w_k = jnp.exp(log_w - m_new)
        decay = jnp.exp(b_last + m - m_new)
        kw = k.astype(F32) * w_k
        c_ref[h] = decay * cmat + lax.dot_general(kw.astype(BF16), v, tn, preferred_element_type=F32)
        n_ref[h] = jnp.broadcast_to(decay * n + jnp.sum(kw, axis=0, keepdims=True), (SUBLANES, MLSTM_DQK))
        m_ref[h] = jnp.broadcast_to(m_new, (SUBLANES, LANES))


def _mlstm(qk, v, o, gcol, grow, ng):
    bsz, seq, _ = qk.shape
    L = MLSTM_CHUNK
    nc = seq // L
    tile = lambda n: pl.BlockSpec((None, L, n), lambda b, c: (b, c, 0))
    return pl.pallas_call(
        _mlstm_kernel,
        grid=(bsz, nc),
        in_specs=[tile(2 * MLSTM_QK), tile(MLSTM_V), tile(MLSTM_V), tile(LANES),
                  pl.BlockSpec((None, None, SUBLANES, L), lambda b, c: (b, c, 0, 0)),
                  _resident(ng.shape)],
        out_specs=tile(MLSTM_V),
        out_shape=jax.ShapeDtypeStruct((bsz, seq, MLSTM_V), BF16),
        scratch_shapes=[pltpu.VMEM((MLSTM_HEADS, MLSTM_DQK, MLSTM_DV), F32),
                        pltpu.VMEM((MLSTM_HEADS, SUBLANES, MLSTM_DQK), F32),
                        pltpu.VMEM((MLSTM_HEADS, SUBLANES, LANES), F32)],
        compiler_params=_params(("arbitrary", "arbitrary")),
        name="mlstm",
    )(qk, v, o, gcol, grow, ng)


def _pair_norm(x, g, is0):
    xx = x * x
    tot = jnp.sum(xx, axis=1, keepdims=True)
    s0 = jnp.sum(jnp.where(is0, xx, 0.0), axis=1, keepdims=True)
    inv0 = lax.rsqrt(s0 * (1.0 / FOX_DH) + EPS)
    inv1 = lax.rsqrt((tot - s0) * (1.0 / FOX_DH) + EPS)
    return x * jnp.where(is0, inv0, inv1) * g


def _fox_kernel(q_ref, k_ref, v_ref, fc_ref, fr_ref, qg_ref, kg_ref, y_ref, kn_ref, acc_ref, m_ref, l_ref, *, blk):
    qi = pl.program_id(2)
    is0 = lax.broadcasted_iota(jnp.int32, (1, LANES), 1) < FOX_DH

    @pl.when(qi == 0)
    def _():
        kn_ref[...] = _pair_norm(k_ref[...].astype(F32), kg_ref[...], is0).astype(BF16)

    qn = _pair_norm(q_ref[...].astype(F32), qg_ref[...], is0) * (FOX_DH ** -0.5)
    qh = (jnp.where(is0, qn, 0.0).astype(BF16), jnp.where(is0, 0.0, qn).astype(BF16))
    fc = fc_ref[...]
    m_ref[...] = jnp.full_like(m_ref, -jnp.inf)
    l_ref[...] = jnp.zeros_like(l_ref)
    acc_ref[...] = jnp.zeros_like(acc_ref)
    nt = (((1,), (1,)), ((), ()))
    row = lax.broadcasted_iota(jnp.int32, (blk, blk), 0)
    col = lax.broadcasted_iota(jnp.int32, (blk, blk), 1)

    def step(j, masked):
        ks = pl.multiple_of(j * blk, blk)
        kb = kn_ref[pl.ds(ks, blk), :]
        vb = v_ref[pl.ds(ks, blk), :]
        fr = fr_ref[:, pl.ds(ks, blk)]
        for h in range(2):
            s = lax.dot_general(qh[h], kb, nt, preferred_element_type=F32) + (fc[:, h:h + 1] - fr[h:h + 1, :])
            if masked:
                s = jnp.where(col <= row, s, -jnp.inf)
            m_prev = m_ref[h]
            m_next = jnp.maximum(m_prev, jnp.max(s, axis=1, keepdims=True))
            alpha = jnp.exp(m_prev - m_next)
            p = jnp.exp(s - m_next[:, 0:1])
            l_ref[h] = alpha * l_ref[h] + jnp.sum(p, axis=1, keepdims=True)
            acc_ref[h] = alpha * acc_ref[h] + jnp.dot(p.astype(BF16), vb, preferred_element_type=F32)
            m_ref[h] = m_next

    def body(j, carry):
        step(j, False)
        return carry

    lax.fori_loop(0, qi, body, 0)
    step(qi, True)
    y_ref[...] = jnp.where(is0, acc_ref[0] / l_ref[0], acc_ref[1] / l_ref[1]).astype(BF16)


def _fox(fox, fcol, frow, qg, kg, blk=256):
    bsz, seq, _ = fox.shape
    npair = FOX_HEADS // 2
    nq = seq // blk
    return pl.pallas_call(
        functools.partial(_fox_kernel, blk=blk),
        grid=(bsz, npair, nq),
        in_specs=[pl.BlockSpec((None, blk, LANES), lambda b, p, i: (b, i, p)),
                  pl.BlockSpec((None, seq, LANES), lambda b, p, i: (b, 0, npair + p)),
                  pl.BlockSpec((None, seq, LANES), lambda b, p, i: (b, 0, 2 * npair + p)),
                  pl.BlockSpec((None, None, blk, 2), lambda b, p, i: (b, p, i, 0)),
                  pl.BlockSpec((None, None, 2, seq), lambda b, p, i: (b, p, 0, 0)),
                  pl.BlockSpec((None, 1, LANES), lambda b, p, i: (p, 0, 0)),
                  pl.BlockSpec((None, 1, LANES), lambda b, p, i: (p, 0, 0))],
        out_specs=pl.BlockSpec((None, blk, LANES), lambda b, p, i: (b, i, p)),
        out_shape=jax.ShapeDtypeStruct((bsz, seq, FOX_W), BF16),
        scratch_shapes=[pltpu.VMEM((seq, LANES), BF16),
                        pltpu.VMEM((2, blk, LANES), F32),
                        pltpu.VMEM((2, blk, LANES), F32),
                        pltpu.VMEM((2, blk, LANES), F32)],
        compiler_params=_params(("arbitrary", "arbitrary", "arbitrary")),
        name="fox",
    )(fox, fox, fox, fcol, frow, qg, kg)


def _merge_kernel(x_ref, ya_ref, yb_ref, ga_ref, gb_ref, gt_ref, wa_ref, wb_ref, wo_ref, o_ref):
    ma = jnp.dot(ya_ref[...], wa_ref[...], preferred_element_type=F32)
    mb = jnp.dot(yb_ref[...], wb_ref[...], preferred_element_type=F32)
    merged = _sigmoid(ga_ref[...].astype(F32)) * ma + _sigmoid(gb_ref[...].astype(F32)) * mb
    o_ref[...] = x_ref[...] + gt_ref[...] * jnp.dot(merged.astype(BF16), wo_ref[...], preferred_element_type=F32)


def _merge(x, ya, yb, g, mod4, wa, wb, wo, tm=512):
    bsz, seq, d = x.shape
    tile = lambda col: pl.BlockSpec((None, tm, d), lambda b, i: (b, i, col))
    return pl.pallas_call(
        _merge_kernel,
        grid=(bsz, seq // tm),
        in_specs=[tile(0), tile(0), tile(0), tile(0), tile(1),
                  pl.BlockSpec((None, None, 1, d), lambda b, i: (b, 5, 0, 0)),
                  _resident(wa.shape), _resident(wb.shape), _resident(wo.shape)],
        out_specs=tile(0),
        out_shape=jax.ShapeDtypeStruct(x.shape, F32),
        compiler_params=_params(("arbitrary", "arbitrary")),
        name="merge",
    )(x, ya, yb, g, g, mod4, wa, wb, wo)


def _ffn_weights(w_in, w_out):
    d = w_in.shape[0]
    nck = D_FF // FF_CHUNK
    a = w_in[:, :D_FF].reshape(d, nck, FF_CHUNK)
    b = w_in[:, D_FF:].reshape(d, nck, FF_CHUNK)
    win3 = jnp.concatenate([a, b], axis=-1).transpose(1, 0, 2).astype(BF16)
    wout3 = w_out.reshape(nck, FF_CHUNK, d).astype(BF16)
    return win3, wout3


def _mix_weights(w_mix, b_mix, conv_w, conv_b):
    offs = _mix_offsets()
    seg = lambda a, i, j: a[..., offs[i]:offs[j]]
    wgate = jnp.concatenate([seg(w_mix, 4, 6), seg(w_mix, 9, 10)], axis=-1)
    wgate = jnp.pad(wgate, ((0, 0), (0, LANES - N_GATES)))
    wgh = wgate.astype(BF16)
    bgate = jnp.pad(jnp.concatenate([seg(b_mix, 4, 6), seg(b_mix, 9, 10)], axis=-1), (0, LANES - N_GATES))
    r = lambda a: a.reshape(1, -1)
    return {
        "wqk": seg(w_mix, 0, 2).astype(BF16), "wv": seg(w_mix, 2, 3).astype(BF16),
        "wo": seg(w_mix, 3, 4).astype(BF16), "wfox": seg(w_mix, 6, 9).astype(BF16),
        "wg": seg(w_mix, 10, 12).astype(BF16), "wgh": wgh, "wgl": (wgate - wgh.astype(F32)).astype(BF16),
        "bqk": r(seg(b_mix, 0, 2)), "bv": r(seg(b_mix, 2, 3)), "bo": r(seg(b_mix, 3, 4)),
        "bfox": r(seg(b_mix, 6, 9)), "bg": r(seg(b_mix, 10, 12)), "bgate": r(bgate),
        "cw": conv_w, "cb": r(conv_b),
    }


def _layer(x, c, w_ada, b_ada, ffn1_norm_g, ffn1_w_in, ffn1_w_out, mix_norm_g, w_mix, b_mix, conv_w, conv_b,
           mlstm_norm_g, fox_q_norm_g, fox_k_norm_g, w_branch_a, w_branch_b, w_out, ffn2_norm_g, ffn2_w_in,
           ffn2_w_out):
    bsz, seq, d = x.shape
    mod4 = _adaln(c, w_ada, b_ada).reshape(bsz, N_MOD, 1, d)

    x = _ffn(x, mod4, 0, ffn1_norm_g, *_ffn_weights(ffn1_w_in, ffn1_w_out))

    qk, v, o, fox, g, gate = _mix(x, mod4, mix_norm_g, _mix_weights(w_mix, b_mix, conv_w, conv_b))
    gcol = _gates(gate)
    nc = seq // MLSTM_CHUNK
    grow = gcol[:, :, :2 * MLSTM_HEADS].reshape(bsz, nc, MLSTM_CHUNK, 2 * MLSTM_HEADS).transpose(0, 1, 3, 2)
    fcum = gcol[:, :, 2 * MLSTM_HEADS:N_GATES].reshape(bsz, seq, FOX_HEADS // 2, 2)
    fcol = fcum.transpose(0, 2, 1, 3)
    frow = fcum.transpose(0, 2, 3, 1)

    ya = _mlstm(qk, v, o, gcol, grow, mlstm_norm_g)
    yb = _fox(fox, fcol, frow, fox_q_norm_g.reshape(FOX_HEADS // 2, 1, LANES),
              fox_k_norm_g.reshape(FOX_HEADS // 2, 1, LANES))
    x = _merge(x, ya, yb, g, mod4, w_branch_a.astype(BF16), w_branch_b.astype(BF16), w_out.astype(BF16))

    return _ffn(x, mod4, 6, ffn2_norm_g, *_ffn_weights(ffn2_w_in, ffn2_w_out))


def kernel(x, c, w_ada, b_ada, ffn1_norm_g, ffn1_w_in, ffn1_w_out, mix_norm_g, w_mix, b_mix, conv_w, conv_b,
           mlstm_norm_g, fox_q_norm_g, fox_k_norm_g, w_branch_a, w_branch_b, w_out, ffn2_norm_g, ffn2_w_in,
           ffn2_w_out):
    for l in range(w_ada.shape[0]):
        x = _layer(x, c, w_ada[l], b_ada[l], ffn1_norm_g[l], ffn1_w_in[l], ffn1_w_out[l], mix_norm_g[l], w_mix[l],
                   b_mix[l], conv_w[l], conv_b[l], mlstm_norm_g[l], fox_q_norm_g[l], fox_k_norm_g[l],
                   w_branch_a[l], w_branch_b[l], w_out[l], ffn2_norm_g[l], ffn2_w_in[l], ffn2_w_out[l])
    return x
```

```python
import functools
import math

import jax
import jax.numpy as jnp
from jax import lax
from jax.experimental import pallas as pl
from jax.experimental.pallas import tpu as pltpu

F32 = jnp.float32
BF16 = jnp.bfloat16
HIGHEST = lax.Precision.HIGHEST

D_MODEL = 1024
D_FF = 2816
MLSTM_HEADS = 4
MLSTM_DQK = 128
MLSTM_DV = 256
MLSTM_CHUNK = 64
CONV_WIDTH = 4
FOX_HEADS = 16
FOX_DH = 64
N_MOD = 9
EPS = 1e-6
MLSTM_QK = MLSTM_HEADS * MLSTM_DQK
MLSTM_V = MLSTM_HEADS * MLSTM_DV
FOX_W = FOX_HEADS * FOX_DH
MIX_SPLITS = (MLSTM_QK, MLSTM_QK, MLSTM_V, MLSTM_V, MLSTM_HEADS, MLSTM_HEADS, FOX_W, FOX_W, FOX_W,
              FOX_HEADS, D_MODEL, D_MODEL)

LANES = 128
SUBLANES = 8
FF_CHUNK = 256
N_GATES = 2 * MLSTM_HEADS + FOX_HEADS
LOG2E = math.log2(math.e)
FOX_VROWS = FOX_DH + 16
VMEM_LIMIT = 56 * 1024 * 1024


def _mix_offsets():
    offs = [0]
    for s in MIX_SPLITS:
        offs.append(offs[-1] + s)
    return offs


def _resident(shape):
    nd = len(shape)
    return pl.BlockSpec(shape, lambda *_: (0,) * nd, pipeline_mode=pl.Buffered(1))


def _params(sem):
    return pltpu.CompilerParams(dimension_semantics=sem, vmem_limit_bytes=VMEM_LIMIT)


def _sigmoid(x):
    return 1.0 / (1.0 + jnp.exp(-x))


def _adaln_kernel(c_ref, w_ref, b_ref, o_ref):
    c = c_ref[...]
    s = c * _sigmoid(c)
    o_ref[...] = jnp.dot(s, w_ref[...], preferred_element_type=F32, precision=HIGHEST) + b_ref[...]


def _adaln(c, w, b):
    bsz, d = c.shape
    n = w.shape[1]
    tn = 1024
    return pl.pallas_call(
        _adaln_kernel,
        grid=(n // tn,),
        in_specs=[pl.BlockSpec((bsz, d), lambda j: (0, 0)),
                  pl.BlockSpec((d, tn), lambda j: (0, j)),
                  pl.BlockSpec((1, tn), lambda j: (0, j))],
        out_specs=pl.BlockSpec((bsz, tn), lambda j: (0, j)),
        out_shape=jax.ShapeDtypeStruct((bsz, n), F32),
        compiler_params=_params(("arbitrary",)),
        name="adaln",
    )(c, w, b.reshape(1, n))


def _modulated_norm(x, g, sc, sh):
    ms = jnp.mean(x * x, axis=-1, keepdims=True)
    return (x * lax.rsqrt(ms + EPS) * g) * (1.0 + sc) + sh


def _ffn_kernel(x_ref, sh_ref, sc_ref, gt_ref, g_ref, win_ref, wout_ref, o_ref, acc_ref):
    x = x_ref[...]
    ub = _modulated_norm(x, g_ref[...], sc_ref[...], sh_ref[...]).astype(BF16)
    acc_ref[...] = jnp.zeros_like(acc_ref)

    def body(j, carry):
        ab = jnp.dot(ub, win_ref[j], preferred_element_type=F32)
        a = ab[:, :FF_CHUNK]
        b = ab[:, FF_CHUNK:]
        h = (a * _sigmoid(a) * b).astype(BF16)
        acc_ref[...] += jnp.dot(h, wout_ref[j], preferred_element_type=F32)
        return carry

    lax.fori_loop(0, win_ref.shape[0], body, 0)
    o_ref[...] = x + (0.5 * gt_ref[...]) * acc_ref[...]


def _ffn(x, mod4, mod_idx, g, win3, wout3, tm=512):
    bsz, seq, d = x.shape
    nt = seq // tm
    row = lambda k: pl.BlockSpec((None, None, 1, d), lambda b, i: (b, k, 0, 0))
    return pl.pallas_call(
        _ffn_kernel,
        grid=(bsz, nt),
        in_specs=[pl.BlockSpec((None, tm, d), lambda b, i: (b, i, 0)),
                  row(mod_idx), row(mod_idx + 1), row(mod_idx + 2),
                  _resident((1, d)), _resident(win3.shape), _resident(wout3.shape)],
        out_specs=pl.BlockSpec((None, tm, d), lambda b, i: (b, i, 0)),
        out_shape=jax.ShapeDtypeStruct(x.shape, F32),
        scratch_shapes=[pltpu.VMEM((tm, d), F32)],
        compiler_params=_params(("arbitrary", "arbitrary")),
        name="ffn",
    )(x, mod4, mod4, mod4, g.reshape(1, d), win3, wout3)


def _mix_kernel(x_ref, sh_ref, sc_ref, g_ref, wqk_ref, wv_ref, wo_ref, wfox_ref, wg_ref, wgh_ref, wgl_ref,
                bqk_ref, bv_ref, bo_ref, bfox_ref, bg_ref, bgate_ref, cw_ref, cb_ref,
                qk_out, v_out, o_out, fox_out, g_out, gate_out, halo_ref):
    i = pl.program_id(1)
    tm = x_ref.shape[0]

    @pl.when(i == 0)
    def _():
        halo_ref[...] = jnp.zeros_like(halo_ref)

    u = _modulated_norm(x_ref[...], g_ref[...], sc_ref[...], sh_ref[...])
    ub = u.astype(BF16)
    ul = (u - ub.astype(F32)).astype(BF16)

    def proj(w_ref, b_ref):
        return jnp.dot(ub, w_ref[...], preferred_element_type=F32) + b_ref[...]

    v_out[...] = proj(wv_ref, bv_ref).astype(BF16)
    o_out[...] = proj(wo_ref, bo_ref).astype(BF16)
    fox_out[...] = proj(wfox_ref, bfox_ref).astype(BF16)
    g_out[...] = proj(wg_ref, bg_ref).astype(BF16)
    gate_out[...] = (jnp.dot(ub, wgh_ref[...], preferred_element_type=F32)
                     + jnp.dot(ul, wgh_ref[...], preferred_element_type=F32)
                     + jnp.dot(ub, wgl_ref[...], preferred_element_type=F32) + bgate_ref[...])

    z = proj(wqk_ref, bqk_ref)
    halo = halo_ref[...]
    rows = lax.broadcasted_iota(jnp.int32, (tm, 1), 0)
    y = z * cw_ref[CONV_WIDTH - 1:CONV_WIDTH, :] + cb_ref[...]
    for k in range(1, CONV_WIDTH):
        zk = pltpu.roll(z, k, 0)
        hk = pltpu.roll(halo, k, 0)
        hk_full = jnp.concatenate([hk, zk[SUBLANES:, :]], axis=0)
        zk = jnp.where(rows < k, hk_full, zk)
        y = y + zk * cw_ref[CONV_WIDTH - 1 - k:CONV_WIDTH - k, :]
    halo_ref[...] = z[tm - SUBLANES:, :]
    y = y * _sigmoid(y)
    lane = lax.broadcasted_iota(jnp.int32, (1, 2 * MLSTM_QK), 1)
    y = y * jnp.where(lane < MLSTM_QK, 1.0, MLSTM_DQK ** -0.5)
    qk_out[...] = y.astype(BF16)


def _mix(x, mod4, g, w, tm=256):
    bsz, seq, d = x.shape
    nt = seq // tm
    row = lambda k: pl.BlockSpec((None, None, 1, d), lambda b, i: (b, k, 0, 0))
    tile = lambda n: pl.BlockSpec((None, tm, n), lambda b, i: (b, i, 0))
    names = ("wqk", "wv", "wo", "wfox", "wg", "wgh", "wgl", "bqk", "bv", "bo", "bfox", "bg", "bgate", "cw", "cb")
    widths = (2 * MLSTM_QK, MLSTM_V, MLSTM_V, 3 * FOX_W, 2 * D_MODEL, LANES)
    dtypes = (BF16, BF16, BF16, BF16, BF16, F32)
    return pl.pallas_call(
        _mix_kernel,
        grid=(bsz, nt),
        in_specs=[tile(d), row(3), row(4), _resident((1, d))] + [_resident(w[n].shape) for n in names],
        out_specs=[tile(n) for n in widths],
        out_shape=[jax.ShapeDtypeStruct((bsz, seq, n), dt) for n, dt in zip(widths, dtypes)],
        scratch_shapes=[pltpu.VMEM((SUBLANES, 2 * MLSTM_QK), F32)],
        compiler_params=_params(("arbitrary", "arbitrary")),
        name="mix",
    )(x, mod4, mod4, g.reshape(1, d), *[w[n] for n in names])


def _split3(x):
    hi = x.astype(BF16).astype(F32)
    r = x - hi
    mid = r.astype(BF16).astype(F32)
    lo = (r - mid).astype(BF16).astype(F32)
    return hi, mid, lo


def _gates_kernel(z_ref, o_ref, kb_ref, carry_ref):
    i = pl.program_id(1)
    tm = z_ref.shape[0]

    @pl.when(i == 0)
    def _():
        carry_ref[...] = jnp.zeros_like(carry_ref)

    z = z_ref[...]
    lf = jnp.minimum(z, 0.0) - jnp.log1p(jnp.exp(-jnp.abs(z)))
    row = lax.broadcasted_iota(jnp.int32, (tm, tm), 0)
    col = lax.broadcasted_iota(jnp.int32, (tm, tm), 1)
    tril = col <= row
    shift = MLSTM_CHUNK.bit_length() - 1
    same_chunk = (row >> shift) == (col >> shift)
    cum_full = jnp.dot(tril.astype(F32), lf, preferred_element_type=F32, precision=HIGHEST) + carry_ref[0:1, :]
    cum_chunk = jnp.dot((tril & same_chunk).astype(F32), lf, preferred_element_type=F32, precision=HIGHEST)
    lane = lax.broadcasted_iota(jnp.int32, (tm, LANES), 1)
    o_ref[...] = jnp.where(lane < MLSTM_HEADS, z, jnp.where(lane < 2 * MLSTM_HEADS, cum_chunk, cum_full))
    carry_ref[...] = jnp.broadcast_to(cum_full[tm - 1:tm, :], carry_ref.shape)

    parts = _split3(cum_full * LOG2E)
    nb = kb_ref.shape[1]
    r = lax.broadcasted_iota(jnp.int32, (LANES, nb), 0)
    c = lax.broadcasted_iota(jnp.int32, (LANES, nb), 1)
    src = 2 * MLSTM_HEADS + 2 * (c >> 7)
    w = c & (LANES - 1)
    kb = jnp.where(w[0:1, :] < 3, 1.0, 0.0)
    for t in range(3):
        sel = ((w == 3 + t) & (r == src)) | ((w == 6 + t) & (r == src + 1))
        kb = kb + jnp.dot(parts[t].astype(BF16), jnp.where(sel, -1.0, 0.0).astype(BF16),
                          preferred_element_type=F32)
    kb_ref[...] = kb.astype(BF16)


def _gates(z, tm=512):
    bsz, seq, n = z.shape
    nb = (FOX_HEADS // 2) * LANES
    return pl.pallas_call(
        _gates_kernel,
        grid=(bsz, seq // tm),
        in_specs=[pl.BlockSpec((None, tm, n), lambda b, i: (b, i, 0))],
        out_specs=[pl.BlockSpec((None, tm, n), lambda b, i: (b, i, 0)),
                   pl.BlockSpec((None, tm, nb), lambda b, i: (b, i, 0))],
        out_shape=[jax.ShapeDtypeStruct(z.shape, F32), jax.ShapeDtypeStruct((bsz, seq, nb), BF16)],
        scratch_shapes=[pltpu.VMEM((SUBLANES, n), F32)],
        compiler_params=_params(("arbitrary", "arbitrary")),
        name="gates",
    )(z)


def _mlstm_kernel(qk_ref, v_ref, o_ref, gcol_ref, grow_ref, ng_ref, y_ref, c_ref, n_ref, m_ref):
    ci = pl.program_id(1)
    L = MLSTM_CHUNK

    @pl.when(ci == 0)
    def _():
        c_ref[...] = jnp.zeros_like(c_ref)
        n_ref[...] = jnp.zeros_like(n_ref)
        m_ref[...] = jnp.zeros_like(m_ref)

    row = lax.broadcasted_iota(jnp.int32, (L, L), 0)
    col = lax.broadcasted_iota(jnp.int32, (L, L), 1)
    causal = col <= row
    gcol = gcol_ref[...]
    grow = grow_ref[...]
    nt = (((1,), (1,)), ((), ()))
    tn = (((0,), (0,)), ((), ()))
    for h in range(MLSTM_HEADS):
        q = qk_ref[:, h * MLSTM_DQK:(h + 1) * MLSTM_DQK]
        k = qk_ref[:, MLSTM_QK + h * MLSTM_DQK:MLSTM_QK + (h + 1) * MLSTM_DQK]
        v = v_ref[:, h * MLSTM_DV:(h + 1) * MLSTM_DV]
        i_c = gcol[:, h:h + 1]
        b_c = gcol[:, MLSTM_HEADS + h:MLSTM_HEADS + h + 1]
        i_r = grow[h:h + 1, :]
        b_r = grow[MLSTM_HEADS + h:MLSTM_HEADS + h + 1, :]
        m = m_ref[h][0:1, 0:1]
        cmat = c_ref[h]
        n = n_ref[h][0:1, :]

        log_d = jnp.where(causal, b_c - b_r + i_r, -jnp.inf)
        log_inter = b_c + m
        m_t = jnp.maximum(log_inter, jnp.max(log_d, axis=1, keepdims=True))
        w_inter = jnp.exp(log_inter - m_t)
        s = lax.dot_general(q, k, nt, preferred_element_type=F32) * jnp.exp(log_d - m_t)
        num = (w_inter * jnp.dot(q, cmat.astype(BF16), preferred_element_type=F32)
               + jnp.dot(s.astype(BF16), v, preferred_element_type=F32))
        den = (w_inter * jnp.sum(q.astype(F32) * n, axis=1, keepdims=True)
               + jnp.sum(s, axis=1, keepdims=True))
        hh = num / jnp.maximum(jnp.abs(den), jnp.exp(-m_t))
        ms = jnp.mean(hh * hh, axis=1, keepdims=True)
        hn = hh * lax.rsqrt(ms + EPS) * ng_ref[h:h + 1, :]
        og = o_ref[:, h * MLSTM_DV:(h + 1) * MLSTM_DV].astype(F32)
        y_ref[:, h * MLSTM_DV:(h + 1) * MLSTM_DV] = (_sigmoid(og) * hn).astype(BF16)

        b_last = b_r[:, L - 1:L]
        log_w = b_last - b_c + i_c
        m_new = jnp.maximum(b_last + m, jnp.max(log_w, axis=0, keepdims=True))
        w_k = jnp.exp(log_w - m_new)
        decay = jnp.exp(b_last + m - m_new)
        kw = k.astype(F32) * w_k
        c_ref[h] = decay * cmat + lax.dot_general(kw.astype(BF16), v, tn, preferred_element_type=F32)
        n_ref[h] = jnp.broadcast_to(decay * n + jnp.sum(kw, axis=0, keepdims=True), (SUBLANES, MLSTM_DQK))
        m_ref[h] = jnp.broadcast_to(m_new, (SUBLANES, LANES))


def _mlstm(qk, v, o, gcol, grow, ng):
    bsz, seq, _ = qk.shape
    L = MLSTM_CHUNK
    nc = seq // L
    tile = lambda n: pl.BlockSpec((None, L, n), lambda b, c: (b, c, 0))
    return pl.pallas_call(
        _mlstm_kernel,
        grid=(bsz, nc),
        in_specs=[tile(2 * MLSTM_QK), tile(MLSTM_V), tile(MLSTM_V), tile(LANES),
                  pl.BlockSpec((None, None, SUBLANES, L), lambda b, c: (b, c, 0, 0)),
                  _resident(ng.shape)],
        out_specs=tile(MLSTM_V),
        out_shape=jax.ShapeDtypeStruct((bsz, seq, MLSTM_V), BF16),
        scratch_shapes=[pltpu.VMEM((MLSTM_HEADS, MLSTM_DQK, MLSTM_DV), F32),
                        pltpu.VMEM((MLSTM_HEADS, SUBLANES, MLSTM_DQK), F32),
                        pltpu.VMEM((MLSTM_HEADS, SUBLANES, LANES), F32)],
        compiler_params=_params(("arbitrary", "arbitrary")),
        name="mlstm",
    )(qk, v, o, gcol, grow, ng)


def _fox_kernel(q_ref, k_ref, v_ref, kb_ref, fr_ref, qg_ref, kg_ref, y_ref,
                kaug_ref, vt_ref, qt_ref, acc_ref, m_ref, sa_ref, sb_ref, *, bq, bk):
    seq = k_ref.shape[0]
    ratio = bq // bk
    nq = seq // bq
    nk = seq // bk

    li = lax.broadcasted_iota(jnp.int32, (LANES, LANES), 0) // FOX_DH
    lj = lax.broadcasted_iota(jnp.int32, (LANES, LANES), 1) // FOX_DH
    same_head = (li == lj).astype(BF16)
    rowv = lax.broadcasted_iota(jnp.int32, (FOX_VROWS - FOX_DH, bk), 0)
    ones_rows = jnp.where(rowv == 0, 1.0, 0.0)
    for jb in range(nk):
        sl = slice(jb * bk, (jb + 1) * bk)
        kf = k_ref[sl, :].astype(F32)
        k2 = kf * kf
        k2h = k2.astype(BF16)
        k2l = (k2 - k2h.astype(F32)).astype(BF16)
        ss = (jnp.dot(k2h, same_head, preferred_element_type=F32)
              + jnp.dot(k2l, same_head, preferred_element_type=F32))
        kaug_ref[sl, :LANES] = (kf * lax.rsqrt(ss * (1.0 / FOX_DH) + EPS) * kg_ref[...]).astype(BF16)
        kaug_ref[sl, LANES:] = kb_ref[sl, :]
        vtt = v_ref[sl, :].astype(F32).T
        for h in range(2):
            vt_ref[jb, h] = jnp.concatenate([vtt[h * FOX_DH:(h + 1) * FOX_DH], ones_rows], axis=0).astype(BF16)

    rowi = lax.broadcasted_iota(jnp.int32, (LANES, 1), 0)
    top = rowi < FOX_DH
    for qi in range(nq):
        sl = slice(qi * bq, (qi + 1) * bq)
        qt = q_ref[sl, :].astype(F32).T
        sq = qt * qt
        inv0 = lax.rsqrt(jnp.sum(sq[:FOX_DH], axis=0, keepdims=True) * (1.0 / FOX_DH) + EPS)
        inv1 = lax.rsqrt(jnp.sum(sq[FOX_DH:], axis=0, keepdims=True) * (1.0 / FOX_DH) + EPS)
        qn = qt * jnp.where(top, inv0, inv1) * (qg_ref[...] * (FOX_DH ** -0.5 * LOG2E))
        fr = fr_ref[:, sl] * LOG2E
        for h in range(2):
            hi, mid, lo = _split3(fr[h:h + 1, :])
            ones = (rowi >= 3 + 3 * h) & (rowi < 6 + 3 * h)
            qaug = jnp.where(rowi == 0, hi, jnp.where(rowi == 1, mid, jnp.where(rowi == 2, lo,
                                                                                jnp.where(ones, 1.0, 0.0))))
            qt_ref[qi, h, :LANES, :] = jnp.where(top == (h == 0), qn, 0.0).astype(BF16)
            qt_ref[qi, h, LANES:, :] = qaug.astype(BF16)

    row = lax.broadcasted_iota(jnp.int32, (bk, bq), 0)
    col = lax.broadcasted_iota(jnp.int32, (bk, bq), 1)
    bufs = (sa_ref, sb_ref)

    def scores(qi, j, s_ref):
        kb = kaug_ref[j * bk:(j + 1) * bk, :]
        for h in range(2):
            s_ref[h] = jnp.dot(kb, qt_ref[qi, h], preferred_element_type=F32)

    def softmax_step(j, mask_off, s_ref):
        for h in range(2):
            st = s_ref[h]
            if mask_off is not None:
                st = jnp.where(row + mask_off <= col, st, -jnp.inf)
            m_prev = m_ref[h]
            m_new = jnp.maximum(m_prev, jnp.max(st, axis=0, keepdims=True))
            alpha = jnp.exp2(m_prev - m_new)
            pt = jnp.exp2(st - m_new).astype(BF16)
            acc_ref[h] = alpha * acc_ref[h] + jnp.dot(vt_ref[j, h], pt, preferred_element_type=F32)
            m_ref[h] = m_new

    blocks = [(qi, j, None if j < ratio * qi else (j - ratio * qi) * bk)
              for qi in range(nq) for j in range(ratio * qi + ratio)]
    scores(blocks[0][0], blocks[0][1], bufs[0])
    for i, (qi, j, mask_off) in enumerate(blocks):
        if j == 0:
            m_ref[...] = jnp.full_like(m_ref, -jnp.inf)
            acc_ref[...] = jnp.zeros_like(acc_ref)
        if i + 1 < len(blocks):
            scores(blocks[i + 1][0], blocks[i + 1][1], bufs[(i + 1) % 2])
        softmax_step(j, mask_off, bufs[i % 2])
        if j == ratio * qi + ratio - 1:
            ot = jnp.concatenate([acc_ref[h, :FOX_DH] / acc_ref[h, FOX_DH:FOX_DH + 1] for h in range(2)], axis=0)
            y_ref[qi * bq:(qi + 1) * bq, :] = ot.T.astype(BF16)


def _fox(fox, kbias, frow, qg, kg, bq=512, bk=256):
    bsz, seq, _ = fox.shape
    npair = FOX_HEADS // 2
    col = lambda off: pl.BlockSpec((None, seq, LANES), lambda b, p: (b, 0, off + p))
    return pl.pallas_call(
        functools.partial(_fox_kernel, bq=bq, bk=bk),
        grid=(bsz, npair),
        in_specs=[col(0), col(npair), col(2 * npair), col(0),
                  pl.BlockSpec((None, None, 2, seq), lambda b, p: (b, p, 0, 0)),
                  pl.BlockSpec((None, LANES, 1), lambda b, p: (p, 0, 0)),
                  pl.BlockSpec((None, 1, LANES), lambda b, p: (p, 0, 0))],
        out_specs=col(0),
        out_shape=jax.ShapeDtypeStruct((bsz, seq, FOX_W), BF16),
        scratch_shapes=[pltpu.VMEM((seq, 2 * LANES), BF16),
                        pltpu.VMEM((seq // bk, 2, FOX_VROWS, bk), BF16),
                        pltpu.VMEM((seq // bq, 2, 2 * LANES, bq), BF16),
                        pltpu.VMEM((2, FOX_VROWS, bq), F32),
                        pltpu.VMEM((2, 1, bq), F32),
                        pltpu.VMEM((2, bk, bq), F32),
                        pltpu.VMEM((2, bk, bq), F32)],
        compiler_params=_params(("arbitrary", "arbitrary")),
        name="fox",
    )(fox, fox, fox, kbias, frow, qg, kg)


def _merge_kernel(x_ref, ya_ref, yb_ref, ga_ref, gb_ref, gt_ref, wa_ref, wb_ref, wo_ref, o_ref):
    ma = jnp.dot(ya_ref[...], wa_ref[...], preferred_element_type=F32)
    mb = jnp.dot(yb_ref[...], wb_ref[...], preferred_element_type=F32)
    merged = _sigmoid(ga_ref[...].astype(F32)) * ma + _sigmoid(gb_ref[...].astype(F32)) * mb
    o_ref[...] = x_ref[...] + gt_ref[...] * jnp.dot(merged.astype(BF16), wo_ref[...], preferred_element_type=F32)


def _merge(x, ya, yb, g, mod4, wa, wb, wo, tm=512):
    bsz, seq, d = x.shape
    tile = lambda col: pl.BlockSpec((None, tm, d), lambda b, i: (b, i, col))
    return pl.pallas_call(
        _merge_kernel,
        grid=(bsz, seq // tm),
        in_specs=[tile(0), tile(0), tile(0), tile(0), tile(1),
                  pl.BlockSpec((None, None, 1, d), lambda b, i: (b, 5, 0, 0)),
                  _resident(wa.shape), _resident(wb.shape), _resident(wo.shape)],
        out_specs=tile(0),
        out_shape=jax.ShapeDtypeStruct(x.shape, F32),
        compiler_params=_params(("arbitrary", "arbitrary")),
        name="merge",
    )(x, ya, yb, g, g, mod4, wa, wb, wo)


def _ffn_weights(w_in, w_out):
    d = w_in.shape[0]
    nck = D_FF // FF_CHUNK
    a = w_in[:, :D_FF].reshape(d, nck, FF_CHUNK)
    b = w_in[:, D_FF:].reshape(d, nck, FF_CHUNK)
    win3 = jnp.concatenate([a, b], axis=-1).transpose(1, 0, 2).astype(BF16)
    wout3 = w_out.reshape(nck, FF_CHUNK, d).astype(BF16)
    return win3, wout3


def _mix_weights(w_mix, b_mix, conv_w, conv_b):
    offs = _mix_offsets()
    seg = lambda a, i, j: a[..., offs[i]:offs[j]]
    wgate = jnp.concatenate([seg(w_mix, 4, 6), seg(w_mix, 9, 10)], axis=-1)
    wgate = jnp.pad(wgate, ((0, 0), (0, LANES - N_GATES)))
    wgh = wgate.astype(BF16)
    bgate = jnp.pad(jnp.concatenate([seg(b_mix, 4, 6), seg(b_mix, 9, 10)], axis=-1), (0, LANES - N_GATES))
    r = lambda a: a.reshape(1, -1)
    return {
        "wqk": seg(w_mix, 0, 2).astype(BF16), "wv": seg(w_mix, 2, 3).astype(BF16),
        "wo": seg(w_mix, 3, 4).astype(BF16), "wfox": seg(w_mix, 6, 9).astype(BF16),
        "wg": seg(w_mix, 10, 12).astype(BF16), "wgh": wgh, "wgl": (wgate - wgh.astype(F32)).astype(BF16),
        "bqk": r(seg(b_mix, 0, 2)), "bv": r(seg(b_mix, 2, 3)), "bo": r(seg(b_mix, 3, 4)),
        "bfox": r(seg(b_mix, 6, 9)), "bg": r(seg(b_mix, 10, 12)), "bgate": r(bgate),
        "cw": conv_w, "cb": r(conv_b),
    }


def _layer(x, c, w_ada, b_ada, ffn1_norm_g, ffn1_w_in, ffn1_w_out, mix_norm_g, w_mix, b_mix, conv_w, conv_b,
           mlstm_norm_g, fox_q_norm_g, fox_k_norm_g, w_branch_a, w_branch_b, w_out, ffn2_norm_g, ffn2_w_in,
           ffn2_w_out):
    bsz, seq, d = x.shape
    mod4 = _adaln(c, w_ada, b_ada).reshape(bsz, N_MOD, 1, d)

    x = _ffn(x, mod4, 0, ffn1_norm_g, *_ffn_weights(ffn1_w_in, ffn1_w_out))

    qk, v, o, fox, g, gate = _mix(x, mod4, mix_norm_g, _mix_weights(w_mix, b_mix, conv_w, conv_b))
    gcol, kbias = _gates(gate)
    nc = seq // MLSTM_CHUNK
    grow = gcol[:, :, :2 * MLSTM_HEADS].reshape(bsz, nc, MLSTM_CHUNK, 2 * MLSTM_HEADS).transpose(0, 1, 3, 2)
    fcum = gcol[:, :, 2 * MLSTM_HEADS:N_GATES].reshape(bsz, seq, FOX_HEADS // 2, 2)
    frow = fcum.transpose(0, 2, 3, 1)

    ya = _mlstm(qk, v, o, gcol, grow, mlstm_norm_g)
    yb = _fox(fox, kbias, frow, fox_q_norm_g.reshape(FOX_HEADS // 2, LANES, 1),
              fox_k_norm_g.reshape(FOX_HEADS // 2, 1, LANES))
    x = _merge(x, ya, yb, g, mod4, w_branch_a.astype(BF16), w_branch_b.astype(BF16), w_out.astype(BF16))

    return _ffn(x, mod4, 6, ffn2_norm_g, *_ffn_weights(ffn2_w_in, ffn2_w_out))


def kernel(x, c, w_ada, b_ada, ffn1_norm_g, ffn1_w_in, ffn1_w_out, mix_norm_g, w_mix, b_mix, conv_w, conv_b,
           mlstm_norm_g, fox_q_norm_g, fox_k_norm_g, w_branch_a, w_branch_b, w_out, ffn2_norm_g, ffn2_w_in,
           ffn2_w_out):
    for l in range(w_ada.shape[0]):
        x = _layer(x, c, w_ada[l], b_ada[l], ffn1_norm_g[l], ffn1_w_in[l], ffn1_w_out[l], mix_norm_g[l], w_mix[l],
                   b_mix[l], conv_w[l], conv_b[l], mlstm_norm_g[l], fox_q_norm_g[l], fox_k_norm_g[l],
                   w_branch_a[l], w_branch_b[l], w_out[l], ffn2_norm_g[l], ffn2_w_in[l], ffn2_w_out[l])
    return x
```

```python
import functools
import math

import jax
import jax.numpy as jnp
from jax import lax
from jax.experimental import pallas as pl
from jax.experimental.pallas import tpu as pltpu

F32 = jnp.float32
BF16 = jnp.bfloat16
HIGHEST = lax.Precision.HIGHEST

D_MODEL = 1024
D_FF = 2816
MLSTM_HEADS = 4
MLSTM_DQK = 128
MLSTM_DV = 256
MLSTM_CHUNK = 256
CONV_WIDTH = 4
FOX_HEADS = 16
FOX_DH = 64
N_MOD = 9
EPS = 1e-6
MLSTM_QK = MLSTM_HEADS * MLSTM_DQK
MLSTM_V = MLSTM_HEADS * MLSTM_DV
FOX_W = FOX_HEADS * FOX_DH
MIX_SPLITS = (MLSTM_QK, MLSTM_QK, MLSTM_V, MLSTM_V, MLSTM_HEADS, MLSTM_HEADS, FOX_W, FOX_W, FOX_W,
              FOX_HEADS, D_MODEL, D_MODEL)

LANES = 128
SUBLANES = 8
FF_CHUNK = 256
N_GATES = 2 * MLSTM_HEADS + FOX_HEADS
LOG2E = math.log2(math.e)
FOX_VROWS = FOX_DH + 16
VMEM_LIMIT = 56 * 1024 * 1024


def _mix_offsets():
    offs = [0]
    for s in MIX_SPLITS:
        offs.append(offs[-1] + s)
    return offs


def _resident(shape):
    nd = len(shape)
    return pl.BlockSpec(shape, lambda *_: (0,) * nd, pipeline_mode=pl.Buffered(1))


def _params(sem):
    return pltpu.CompilerParams(dimension_semantics=sem, vmem_limit_bytes=VMEM_LIMIT)


def _sigmoid(x):
    return 1.0 / (1.0 + jnp.exp(-x))


def _adaln_kernel(c_ref, w_ref, b_ref, o_ref):
    c = c_ref[...]
    s = c * _sigmoid(c)
    o_ref[...] = jnp.dot(s, w_ref[...], preferred_element_type=F32, precision=HIGHEST) + b_ref[...]


def _adaln(c, w, b):
    bsz, d = c.shape
    n = w.shape[1]
    tn = 1024
    return pl.pallas_call(
        _adaln_kernel,
        grid=(n // tn,),
        in_specs=[pl.BlockSpec((bsz, d), lambda j: (0, 0)),
                  pl.BlockSpec((d, tn), lambda j: (0, j)),
                  pl.BlockSpec((1, tn), lambda j: (0, j))],
        out_specs=pl.BlockSpec((bsz, tn), lambda j: (0, j)),
        out_shape=jax.ShapeDtypeStruct((bsz, n), F32),
        compiler_params=_params(("arbitrary",)),
        name="adaln",
    )(c, w, b.reshape(1, n))


def _modulated_norm(x, g, sc, sh):
    ms = jnp.mean(x * x, axis=-1, keepdims=True)
    return (x * lax.rsqrt(ms + EPS) * g) * (1.0 + sc) + sh


def _ffn_kernel(x_ref, sh_ref, sc_ref, gt_ref, g_ref, win_ref, wout_ref, o_ref):
    x = x_ref[...]
    ub = _modulated_norm(x, g_ref[...], sc_ref[...], sh_ref[...]).astype(BF16)
    acc = None
    for j in range(win_ref.shape[0]):
        ab = jnp.dot(ub, win_ref[j], preferred_element_type=F32)
        a = ab[:, :FF_CHUNK]
        b = ab[:, FF_CHUNK:]
        h = (a * _sigmoid(a) * b).astype(BF16)
        d = jnp.dot(h, wout_ref[j], preferred_element_type=F32)
        acc = d if acc is None else acc + d
    o_ref[...] = x + (0.5 * gt_ref[...]) * acc


def _ffn(x, mod4, mod_idx, g, win3, wout3, tm=512):
    bsz, seq, d = x.shape
    nt = seq // tm
    row = lambda k: pl.BlockSpec((None, None, 1, d), lambda b, i: (b, k, 0, 0))
    return pl.pallas_call(
        _ffn_kernel,
        grid=(bsz, nt),
        in_specs=[pl.BlockSpec((None, tm, d), lambda b, i: (b, i, 0)),
                  row(mod_idx), row(mod_idx + 1), row(mod_idx + 2),
                  _resident((1, d)), _resident(win3.shape), _resident(wout3.shape)],
        out_specs=pl.BlockSpec((None, tm, d), lambda b, i: (b, i, 0)),
        out_shape=jax.ShapeDtypeStruct(x.shape, F32),
        compiler_params=_params(("arbitrary", "arbitrary")),
        name="ffn",
    )(x, mod4, mod4, mod4, g.reshape(1, d), win3, wout3)


def _mix_kernel(x_ref, sh_ref, sc_ref, g_ref, wqk_ref, wv_ref, wo_ref, wfox_ref, wg_ref, wgate_ref,
                bqk_ref, bv_ref, bo_ref, bfox_ref, bg_ref, bgate_ref, cw_ref, cb_ref,
                q_out, kt_out, v_out, o_out, fox_out, g_out, gate_out, halo_ref):
    i = pl.program_id(1)
    tm = x_ref.shape[0]

    @pl.when(i == 0)
    def _():
        halo_ref[...] = jnp.zeros_like(halo_ref)

    ub = _modulated_norm(x_ref[...], g_ref[...], sc_ref[...], sh_ref[...]).astype(BF16)

    def proj(w_ref, b_ref):
        return jnp.dot(ub, w_ref[...], preferred_element_type=F32) + b_ref[...]

    z = proj(wqk_ref, bqk_ref)
    gate_out[...] = proj(wgate_ref, bgate_ref)
    v_out[...] = proj(wv_ref, bv_ref).astype(BF16)
    o_out[...] = proj(wo_ref, bo_ref).astype(BF16)
    fox_out[...] = proj(wfox_ref, bfox_ref).astype(BF16)
    g_out[...] = proj(wg_ref, bg_ref).astype(BF16)
    halo = halo_ref[...]
    rows = lax.broadcasted_iota(jnp.int32, (tm, 1), 0)
    y = z * cw_ref[CONV_WIDTH - 1:CONV_WIDTH, :] + cb_ref[...]
    for k in range(1, CONV_WIDTH):
        zk = pltpu.roll(z, k, 0)
        hk = pltpu.roll(halo, k, 0)
        hk_full = jnp.concatenate([hk, zk[SUBLANES:, :]], axis=0)
        zk = jnp.where(rows < k, hk_full, zk)
        y = y + zk * cw_ref[CONV_WIDTH - 1 - k:CONV_WIDTH - k, :]
    halo_ref[...] = z[tm - SUBLANES:, :]
    y = y * _sigmoid(y)
    lane = lax.broadcasted_iota(jnp.int32, (1, 2 * MLSTM_QK), 1)
    y = y * jnp.where(lane < MLSTM_QK, 1.0, MLSTM_DQK ** -0.5)
    q_out[...] = y[:, :MLSTM_QK].astype(BF16)
    kt_out[...] = y[:, MLSTM_QK:].T.astype(BF16)


def _mix(x, mod4, g, w, tm=256):
    bsz, seq, d = x.shape
    nt = seq // tm
    row = lambda k: pl.BlockSpec((None, None, 1, d), lambda b, i: (b, k, 0, 0))
    tile = lambda n: pl.BlockSpec((None, tm, n), lambda b, i: (b, i, 0))
    names = ("wqk", "wv", "wo", "wfox", "wg", "wgate", "bqk", "bv", "bo", "bfox", "bg", "bgate", "cw", "cb")
    widths = (MLSTM_V, MLSTM_V, 3 * FOX_W, 2 * D_MODEL, LANES)
    dtypes = (BF16, BF16, BF16, BF16, F32)
    return pl.pallas_call(
        _mix_kernel,
        grid=(bsz, nt),
        in_specs=[tile(d), row(3), row(4), _resident((1, d))] + [_resident(w[n].shape) for n in names],
        out_specs=[tile(MLSTM_QK), pl.BlockSpec((None, MLSTM_QK, tm), lambda b, i: (b, 0, i))]
        + [tile(n) for n in widths],
        out_shape=[jax.ShapeDtypeStruct((bsz, seq, MLSTM_QK), BF16), jax.ShapeDtypeStruct((bsz, MLSTM_QK, seq), BF16)]
        + [jax.ShapeDtypeStruct((bsz, seq, n), dt) for n, dt in zip(widths, dtypes)],
        scratch_shapes=[pltpu.VMEM((SUBLANES, 2 * MLSTM_QK), F32)],
        compiler_params=_params(("arbitrary", "arbitrary")),
        name="mix",
    )(x, mod4, mod4, g.reshape(1, d), *[w[n] for n in names])


def _split3(x):
    hi = x.astype(BF16).astype(F32)
    r = x - hi
    mid = r.astype(BF16).astype(F32)
    lo = (r - mid).astype(BF16).astype(F32)
    return hi, mid, lo


def _gates_kernel(z_ref, g_ref, kb_ref, mtb_ref, bb_ref, sc_ref, carry_ref):
    ci = pl.program_id(1)
    tm = z_ref.shape[0]

    @pl.when(ci == 0)
    def _():
        carry_ref[...] = jnp.zeros_like(carry_ref)

    z = z_ref[...]
    lane = lax.broadcasted_iota(jnp.int32, (1, LANES), 1)
    lf = jnp.minimum(z, 0.0) - jnp.log1p(jnp.exp(-jnp.abs(z)))
    row = lax.broadcasted_iota(jnp.int32, (tm, tm), 0)
    col = lax.broadcasted_iota(jnp.int32, (tm, tm), 1)
    lf3 = jnp.concatenate([p.astype(BF16) for p in _split3(lf)], axis=1)
    c3 = jnp.dot((col <= row).astype(BF16), lf3, preferred_element_type=F32)
    cum = c3[:, :LANES] + c3[:, LANES:2 * LANES] + c3[:, 2 * LANES:]
    fcum = cum + carry_ref[0:1, :]
    carry_ref[0:1, :] = fcum[tm - 1:tm, :]

    a = pltpu.roll(z, MLSTM_HEADS, 1) - cum
    rows = lax.broadcasted_iota(jnp.int32, (tm, 1), 0)
    cmax = a
    k = 1
    while k < tm:
        cmax = jnp.where(rows >= k, jnp.maximum(cmax, pltpu.roll(cmax, k, 0)), cmax)
        k *= 2
    m = carry_ref[1:2, :]
    mt = jnp.maximum(m, cmax)
    b_last = cum[tm - 1:tm, :]
    m_new = b_last + jnp.maximum(m, cmax[tm - 1:tm, :])
    carry_ref[1:2, :] = m_new
    g_ref[...] = jnp.where(lane < MLSTM_HEADS, z, jnp.where(lane < 2 * MLSTM_HEADS, a, fcum))
    sc_ref[...] = jnp.concatenate([m, b_last + m - m_new, b_last - m_new,
                                   jnp.zeros((SUBLANES - 3, LANES), F32)], axis=0)

    nrep = mtb_ref.shape[1]
    r = lax.broadcasted_iota(jnp.int32, (LANES, nrep), 0)
    c = lax.broadcasted_iota(jnp.int32, (LANES, nrep), 1)
    sel = jnp.where(r == MLSTM_HEADS + (c >> 7), 1.0, 0.0).astype(BF16)

    def replicate(x, nparts):
        return sum(jnp.dot(p.astype(BF16), sel, preferred_element_type=F32) for p in _split3(x)[:nparts])

    mtb_ref[...] = replicate(mt, 1)
    bb_ref[...] = replicate(cum, 2)

    parts = _split3(fcum * LOG2E)
    nb = kb_ref.shape[1]
    r = lax.broadcasted_iota(jnp.int32, (LANES, nb), 0)
    c = lax.broadcasted_iota(jnp.int32, (LANES, nb), 1)
    src = 2 * MLSTM_HEADS + 2 * (c >> 7)
    w = c & (LANES - 1)
    selb = jnp.concatenate(
        [jnp.where(((w == 3 + t) & (r == src)) | ((w == 6 + t) & (r == src + 1)), -1.0, 0.0).astype(BF16)
         for t in range(3)], axis=0)
    parts3 = jnp.concatenate([p.astype(BF16) for p in parts], axis=1)
    kb = jnp.where(w[0:1, :] < 3, 1.0, 0.0) + jnp.dot(parts3, selb, preferred_element_type=F32)
    kb_ref[...] = kb.astype(BF16)


def _gates(z):
    bsz, seq, n = z.shape
    tm = MLSTM_CHUNK
    nb = (FOX_HEADS // 2) * LANES
    nrep = MLSTM_HEADS * LANES
    tile = lambda w: pl.BlockSpec((None, tm, w), lambda b, i: (b, i, 0))
    return pl.pallas_call(
        _gates_kernel,
        grid=(bsz, seq // tm),
        in_specs=[tile(n)],
        out_specs=[tile(n), tile(nb), tile(nrep), tile(nrep),
                   pl.BlockSpec((None, None, SUBLANES, LANES), lambda b, i: (b, i, 0, 0))],
        out_shape=[jax.ShapeDtypeStruct(z.shape, F32), jax.ShapeDtypeStruct((bsz, seq, nb), BF16),
                   jax.ShapeDtypeStruct((bsz, seq, nrep), F32), jax.ShapeDtypeStruct((bsz, seq, nrep), F32),
                   jax.ShapeDtypeStruct((bsz, seq // tm, SUBLANES, LANES), F32)],
        scratch_shapes=[pltpu.VMEM((SUBLANES, n), F32)],
        compiler_params=_params(("arbitrary", "arbitrary")),
        name="gates",
    )(z)


def _mlstm_kernel(q_ref, kt_ref, v_ref, o_ref, mtb_ref, bb_ref, arow_ref, sc_ref, ng_ref, y_ref, c_ref, n_ref):
    ci = pl.program_id(1)
    L = q_ref.shape[0]

    @pl.when(ci == 0)
    def _():
        c_ref[...] = jnp.zeros_like(c_ref)
        n_ref[...] = jnp.zeros_like(n_ref)

    row = lax.broadcasted_iota(jnp.int32, (L, L), 0)
    col = lax.broadcasted_iota(jnp.int32, (L, L), 1)
    causal = col <= row
    ones_l = jnp.ones((L, LANES), BF16)
    ones_v = jnp.ones((MLSTM_DV, LANES), BF16)
    arow = arow_ref[...]
    sc = sc_ref[...]
    for h in range(MLSTM_HEADS):
        q = q_ref[:, h * MLSTM_DQK:(h + 1) * MLSTM_DQK]
        kt = kt_ref[h * MLSTM_DQK:(h + 1) * MLSTM_DQK, :]
        v = v_ref[:, h * MLSTM_DV:(h + 1) * MLSTM_DV]
        mtb = mtb_ref[:, h * LANES:(h + 1) * LANES]
        bb = bb_ref[:, h * LANES:(h + 1) * LANES]
        a_r = arow[h:h + 1, :]
        lh = MLSTM_HEADS + h
        m, dec, wko = sc[0:1, lh:lh + 1], sc[1:2, lh:lh + 1], sc[2:3, lh:lh + 1]
        cmat = c_ref[h]
        nmat = n_ref[h]

        mt2 = jnp.concatenate([mtb] * (L // LANES), axis=1)
        p = jnp.exp(jnp.where(causal, a_r - mt2, -jnp.inf))
        s = (jnp.dot(q, kt, preferred_element_type=F32) * p).astype(BF16)
        w_inter = jnp.exp(m - mtb)
        qw = (q.astype(F32) * w_inter).astype(BF16)
        num = (jnp.dot(s, v, preferred_element_type=F32)
               + jnp.dot(qw, cmat.astype(BF16), preferred_element_type=F32))
        den = (w_inter * jnp.dot(q, nmat.astype(BF16), preferred_element_type=F32)
               + jnp.dot(s, ones_l, preferred_element_type=F32))
        dmax = jnp.maximum(jnp.abs(den), jnp.exp(-bb - mtb))
        msn = jnp.dot((num * num).astype(BF16), ones_v, preferred_element_type=F32) * (1.0 / MLSTM_DV)
        r = lax.rsqrt(msn + EPS * dmax * dmax)
        r2 = jnp.concatenate([r] * (MLSTM_DV // LANES), axis=1)
        og = o_ref[:, h * MLSTM_DV:(h + 1) * MLSTM_DV].astype(F32)
        y_ref[:, h * MLSTM_DV:(h + 1) * MLSTM_DV] = (_sigmoid(og) * (num * r2 * ng_ref[h:h + 1, :])).astype(BF16)

        kwt = (kt.astype(F32) * jnp.exp(a_r + wko)).astype(BF16)
        decay = jnp.exp(dec)
        c_ref[h] = decay * cmat + jnp.dot(kwt, v, preferred_element_type=F32)
        n_ref[h] = decay * nmat + jnp.dot(kwt, ones_l, preferred_element_type=F32)


def _mlstm(q, kt, v, o, mtb, bb, arow, sc, ng):
    bsz, seq, _ = q.shape
    L = MLSTM_CHUNK
    nc = seq // L
    tile = lambda n: pl.BlockSpec((None, L, n), lambda b, c: (b, c, 0))
    return pl.pallas_call(
        _mlstm_kernel,
        grid=(bsz, nc),
        in_specs=[tile(MLSTM_QK), pl.BlockSpec((None, MLSTM_QK, L), lambda b, c: (b, 0, c)),
                  tile(MLSTM_V), tile(MLSTM_V), tile(MLSTM_HEADS * LANES), tile(MLSTM_HEADS * LANES),
                  pl.BlockSpec((None, None, MLSTM_HEADS, L), lambda b, c: (b, c, 0, 0)),
                  pl.BlockSpec((None, None, SUBLANES, LANES), lambda b, c: (b, c, 0, 0)),
                  _resident(ng.shape)],
        out_specs=tile(MLSTM_V),
        out_shape=jax.ShapeDtypeStruct((bsz, seq, MLSTM_V), BF16),
        scratch_shapes=[pltpu.VMEM((MLSTM_HEADS, MLSTM_DQK, MLSTM_DV), F32),
                        pltpu.VMEM((MLSTM_HEADS, MLSTM_DQK, LANES), F32)],
        compiler_params=_params(("arbitrary", "arbitrary")),
        name="mlstm",
    )(q, kt, v, o, mtb, bb, arow, sc, ng)


def _fox_kernel(q_ref, k_ref, v_ref, kb_ref, fr_ref, qg_ref, kg_ref, y_ref,
                kaug_ref, vt_ref, qt_ref, acc_ref, m_ref, sa_ref, sb_ref, *, bq, bk):
    seq = k_ref.shape[0]
    ratio = bq // bk
    nq = seq // bq
    nk = seq // bk

    li = lax.broadcasted_iota(jnp.int32, (LANES, LANES), 0) // FOX_DH
    lj = lax.broadcasted_iota(jnp.int32, (LANES, LANES), 1) // FOX_DH
    same_head = (li == lj).astype(BF16)
    rowv = lax.broadcasted_iota(jnp.int32, (FOX_VROWS - FOX_DH, bk), 0)
    ones_rows = jnp.where(rowv == 0, 1.0, 0.0)
    for jb in range(nk):
        sl = slice(jb * bk, (jb + 1) * bk)
        kf = k_ref[sl, :].astype(F32)
        k2 = kf * kf
        k2h = k2.astype(BF16)
        k2l = (k2 - k2h.astype(F32)).astype(BF16)
        ss = (jnp.dot(k2h, same_head, preferred_element_type=F32)
              + jnp.dot(k2l, same_head, preferred_element_type=F32))
        kaug_ref[sl, :LANES] = (kf * lax.rsqrt(ss * (1.0 / FOX_DH) + EPS) * kg_ref[...]).astype(BF16)
        kaug_ref[sl, LANES:] = kb_ref[sl, :]
        vtt = v_ref[sl, :].astype(F32).T
        for h in range(2):
            vt_ref[jb, h] = jnp.concatenate([vtt[h * FOX_DH:(h + 1) * FOX_DH], ones_rows], axis=0).astype(BF16)

    rowi = lax.broadcasted_iota(jnp.int32, (LANES, 1), 0)
    top = rowi < FOX_DH
    for qi in range(nq):
        sl = slice(qi * bq, (qi + 1) * bq)
        qt = q_ref[sl, :].astype(F32).T
        sq = qt * qt
        inv0 = lax.rsqrt(jnp.sum(sq[:FOX_DH], axis=0, keepdims=True) * (1.0 / FOX_DH) + EPS)
        inv1 = lax.rsqrt(jnp.sum(sq[FOX_DH:], axis=0, keepdims=True) * (1.0 / FOX_DH) + EPS)
        qn = qt * jnp.where(top, inv0, inv1) * (qg_ref[...] * (FOX_DH ** -0.5 * LOG2E))
        fr = fr_ref[:, sl] * LOG2E
        for h in range(2):
            hi, mid, lo = _split3(fr[h:h + 1, :])
            ones = (rowi >= 3 + 3 * h) & (rowi < 6 + 3 * h)
            qaug = jnp.where(rowi == 0, hi, jnp.where(rowi == 1, mid, jnp.where(rowi == 2, lo,
                                                                                jnp.where(ones, 1.0, 0.0))))
            qt_ref[qi, h, :LANES, :] = jnp.where(top == (h == 0), qn, 0.0).astype(BF16)
            qt_ref[qi, h, LANES:, :] = qaug.astype(BF16)

    row = lax.broadcasted_iota(jnp.int32, (bk, bq), 0)
    col = lax.broadcasted_iota(jnp.int32, (bk, bq), 1)
    bufs = (sa_ref, sb_ref)

    def scores(qi, j, s_ref):
        kb = kaug_ref[j * bk:(j + 1) * bk, :]
        for h in range(2):
            s_ref[h] = jnp.dot(kb, qt_ref[qi, h], preferred_element_type=F32)

    def softmax_step(j, mask_off, s_ref):
        for h in range(2):
            st = s_ref[h]
            if mask_off is not None:
                st = jnp.where(row + mask_off <= col, st, -jnp.inf)
            m_prev = m_ref[h]
            m_new = jnp.maximum(m_prev, jnp.max(st, axis=0, keepdims=True))
            alpha = jnp.exp2(m_prev - m_new)
            pt = jnp.exp2(st - m_new).astype(BF16)
            acc_ref[h] = alpha * acc_ref[h] + jnp.dot(vt_ref[j, h], pt, preferred_element_type=F32)
            m_ref[h] = m_new

    blocks = [(qi, j, None if j < ratio * qi else (j - ratio * qi) * bk)
              for qi in range(nq) for j in range(ratio * qi + ratio)]
    scores(blocks[0][0], blocks[0][1], bufs[0])
    for i, (qi, j, mask_off) in enumerate(blocks):
        if j == 0:
            m_ref[...] = jnp.full_like(m_ref, -jnp.inf)
            acc_ref[...] = jnp.zeros_like(acc_ref)
        if i + 1 < len(blocks):
            scores(blocks[i + 1][0], blocks[i + 1][1], bufs[(i + 1) % 2])
        softmax_step(j, mask_off, bufs[i % 2])
        if j == ratio * qi + ratio - 1:
            ot = jnp.concatenate([acc_ref[h, :FOX_DH] / acc_ref[h, FOX_DH:FOX_DH + 1] for h in range(2)], axis=0)
            y_ref[qi * bq:(qi + 1) * bq, :] = ot.T.astype(BF16)


def _fox(fox, kbias, frow, qg, kg, bq=512, bk=256):
    bsz, seq, _ = fox.shape
    npair = FOX_HEADS // 2
    col = lambda off: pl.BlockSpec((None, seq, LANES), lambda b, p: (b, 0, off + p))
    return pl.pallas_call(
        functools.partial(_fox_kernel, bq=bq, bk=bk),
        grid=(bsz, npair),
        in_specs=[col(0), col(npair), col(2 * npair), col(0),
                  pl.BlockSpec((None, None, 2, seq), lambda b, p: (b, p, 0, 0)),
                  pl.BlockSpec((None, LANES, 1), lambda b, p: (p, 0, 0)),
                  pl.BlockSpec((None, 1, LANES), lambda b, p: (p, 0, 0))],
        out_specs=col(0),
        out_shape=jax.ShapeDtypeStruct((bsz, seq, FOX_W), BF16),
        scratch_shapes=[pltpu.VMEM((seq, 2 * LANES), BF16),
                        pltpu.VMEM((seq // bk, 2, FOX_VROWS, bk), BF16),
                        pltpu.VMEM((seq // bq, 2, 2 * LANES, bq), BF16),
                        pltpu.VMEM((2, FOX_VROWS, bq), F32),
                        pltpu.VMEM((2, 1, bq), F32),
                        pltpu.VMEM((2, bk, bq), F32),
                        pltpu.VMEM((2, bk, bq), F32)],
        compiler_params=_params(("arbitrary", "arbitrary")),
        name="fox",
    )(fox, fox, fox, kbias, frow, qg, kg)


def _merge_kernel(x_ref, ya_ref, yb_ref, ga_ref, gb_ref, gt_ref, wa_ref, wb_ref, wo_ref, o_ref):
    ma = jnp.dot(ya_ref[...], wa_ref[...], preferred_element_type=F32)
    mb = jnp.dot(yb_ref[...], wb_ref[...], preferred_element_type=F32)
    merged = _sigmoid(ga_ref[...].astype(F32)) * ma + _sigmoid(gb_ref[...].astype(F32)) * mb
    o_ref[...] = x_ref[...] + gt_ref[...] * jnp.dot(merged.astype(BF16), wo_ref[...], preferred_element_type=F32)


def _merge(x, ya, yb, g, mod4, wa, wb, wo, tm=512):
    bsz, seq, d = x.shape
    tile = lambda col: pl.BlockSpec((None, tm, d), lambda b, i: (b, i, col))
    return pl.pallas_call(
        _merge_kernel,
        grid=(bsz, seq // tm),
        in_specs=[tile(0), tile(0), tile(0), tile(0), tile(1),
                  pl.BlockSpec((None, None, 1, d), lambda b, i: (b, 5, 0, 0)),
                  _resident(wa.shape), _resident(wb.shape), _resident(wo.shape)],
        out_specs=tile(0),
        out_shape=jax.ShapeDtypeStruct(x.shape, F32),
        compiler_params=_params(("arbitrary", "arbitrary")),
        name="merge",
    )(x, ya, yb, g, g, mod4, wa, wb, wo)


def _ffn_weights(w_in, w_out):
    d = w_in.shape[0]
    nck = D_FF // FF_CHUNK
    a = w_in[:, :D_FF].reshape(d, nck, FF_CHUNK)
    b = w_in[:, D_FF:].reshape(d, nck, FF_CHUNK)
    win3 = jnp.concatenate([a, b], axis=-1).transpose(1, 0, 2).astype(BF16)
    wout3 = w_out.reshape(nck, FF_CHUNK, d).astype(BF16)
    return win3, wout3


def _mix_weights(w_mix, b_mix, conv_w, conv_b):
    offs = _mix_offsets()
    seg = lambda a, i, j: a[..., offs[i]:offs[j]]
    wgate = jnp.concatenate([seg(w_mix, 4, 6), seg(w_mix, 9, 10)], axis=-1)
    wgate = jnp.pad(wgate, ((0, 0), (0, LANES - N_GATES))).astype(BF16)
    bgate = jnp.pad(jnp.concatenate([seg(b_mix, 4, 6), seg(b_mix, 9, 10)], axis=-1), (0, LANES - N_GATES))
    r = lambda a: a.reshape(1, -1)
    return {
        "wqk": seg(w_mix, 0, 2).astype(BF16), "wv": seg(w_mix, 2, 3).astype(BF16),
        "wo": seg(w_mix, 3, 4).astype(BF16), "wfox": seg(w_mix, 6, 9).astype(BF16),
        "wg": seg(w_mix, 10, 12).astype(BF16), "wgate": wgate,
        "bqk": r(seg(b_mix, 0, 2)), "bv": r(seg(b_mix, 2, 3)), "bo": r(seg(b_mix, 3, 4)),
        "bfox": r(seg(b_mix, 6, 9)), "bg": r(seg(b_mix, 10, 12)), "bgate": r(bgate),
        "cw": conv_w, "cb": r(conv_b),
    }


def _layer(x, c, w_ada, b_ada, ffn1_norm_g, ffn1_w_in, ffn1_w_out, mix_norm_g, w_mix, b_mix, conv_w, conv_b,
           mlstm_norm_g, fox_q_norm_g, fox_k_norm_g, w_branch_a, w_branch_b, w_out, ffn2_norm_g, ffn2_w_in,
           ffn2_w_out):
    bsz, seq, d = x.shape
    mod4 = _adaln(c, w_ada, b_ada).reshape(bsz, N_MOD, 1, d)

    x = _ffn(x, mod4, 0, ffn1_norm_g, *_ffn_weights(ffn1_w_in, ffn1_w_out))

    q, kt, v, o, fox, g, gate = _mix(x, mod4, mix_norm_g, _mix_weights(w_mix, b_mix, conv_w, conv_b))
    gcol, kbias, mtb, bb, sc = _gates(gate)
    nc = seq // MLSTM_CHUNK
    arow = gcol[:, :, MLSTM_HEADS:2 * MLSTM_HEADS].reshape(bsz, nc, MLSTM_CHUNK, MLSTM_HEADS).transpose(0, 1, 3, 2)
    fcum = gcol[:, :, 2 * MLSTM_HEADS:N_GATES].reshape(bsz, seq, FOX_HEADS // 2, 2)
    frow = fcum.transpose(0, 2, 3, 1)

    ya = _mlstm(q, kt, v, o, mtb, bb, arow, sc, mlstm_norm_g)
    yb = _fox(fox, kbias, frow, fox_q_norm_g.reshape(FOX_HEADS // 2, LANES, 1),
              fox_k_norm_g.reshape(FOX_HEADS // 2, 1, LANES))
    x = _merge(x, ya, yb, g, mod4, w_branch_a.astype(BF16), w_branch_b.astype(BF16), w_out.astype(BF16))

    return _ffn(x, mod4, 6, ffn2_norm_g, *_ffn_weights(ffn2_w_in, ffn2_w_out))


def kernel(x, c, w_ada, b_ada, ffn1_norm_g, ffn1_w_in, ffn1_w_out, mix_norm_g, w_mix, b_mix, conv_w, conv_b,
           mlstm_norm_g, fox_q_norm_g, fox_k_norm_g, w_branch_a, w_branch_b, w_out, ffn2_norm_g, ffn2_w_in,
           ffn2_w_out):
    for l in range(w_ada.shape[0]):
        x = _layer(x, c, w_ada[l], b_ada[l], ffn1_norm_g[l], ffn1_w_in[l], ffn1_w_out[l], mix_norm_g[l], w_mix[l],
                   b_mix[l], conv_w[l], conv_b[l], mlstm_norm_g[l], fox_q_norm_g[l], fox_k_norm_g[l],
                   w_branch_a[l], w_branch_b[l], w_out[l], ffn2_norm_g[l], ffn2_w_in[l], ffn2_w_out[l])
    return x
```

```python
import functools
import math

import jax
import jax.numpy as jnp
from jax import lax
from jax.experimental import pallas as pl
from jax.experimental.pallas import tpu as pltpu

F32 = jnp.float32
BF16 = jnp.bfloat16
HIGHEST = lax.Precision.HIGHEST

D_MODEL = 1024
D_FF = 2816
MLSTM_HEADS = 4
MLSTM_DQK = 128
MLSTM_DV = 256
MLSTM_CHUNK = 256
CONV_WIDTH = 4
FOX_HEADS = 16
FOX_DH = 64
N_MOD = 9
EPS = 1e-6
MLSTM_QK = MLSTM_HEADS * MLSTM_DQK
MLSTM_V = MLSTM_HEADS * MLSTM_DV
FOX_W = FOX_HEADS * FOX_DH
MIX_SPLITS = (MLSTM_QK, MLSTM_QK, MLSTM_V, MLSTM_V, MLSTM_HEADS, MLSTM_HEADS, FOX_W, FOX_W, FOX_W,
              FOX_HEADS, D_MODEL, D_MODEL)

LANES = 128
SUBLANES = 8
FF_CHUNK = 256
N_GATES = 2 * MLSTM_HEADS + FOX_HEADS
LOG2E = math.log2(math.e)
FOX_VROWS = FOX_DH + 16
VMEM_LIMIT = 56 * 1024 * 1024


def _mix_offsets():
    offs = [0]
    for s in MIX_SPLITS:
        offs.append(offs[-1] + s)
    return offs


def _resident(shape):
    nd = len(shape)
    return pl.BlockSpec(shape, lambda *_: (0,) * nd, pipeline_mode=pl.Buffered(1))


def _params(sem):
    return pltpu.CompilerParams(dimension_semantics=sem, vmem_limit_bytes=VMEM_LIMIT)


def _sigmoid(x):
    return 1.0 / (1.0 + jnp.exp(-x))


def _adaln_kernel(c_ref, w_ref, b_ref, o_ref):
    c = c_ref[...]
    s = c * _sigmoid(c)
    o_ref[...] = jnp.dot(s, w_ref[...], preferred_element_type=F32, precision=HIGHEST) + b_ref[...]


def _adaln(c, w, b):
    bsz, d = c.shape
    n = w.shape[1]
    tn = 1024
    return pl.pallas_call(
        _adaln_kernel,
        grid=(n // tn,),
        in_specs=[pl.BlockSpec((bsz, d), lambda j: (0, 0)),
                  pl.BlockSpec((d, tn), lambda j: (0, j)),
                  pl.BlockSpec((1, tn), lambda j: (0, j))],
        out_specs=pl.BlockSpec((bsz, tn), lambda j: (0, j)),
        out_shape=jax.ShapeDtypeStruct((bsz, n), F32),
        compiler_params=_params(("arbitrary",)),
        name="adaln",
    )(c, w, b.reshape(1, n))


def _modulated_norm(x, g, sc, sh):
    ms = jnp.mean(x * x, axis=-1, keepdims=True)
    return (x * lax.rsqrt(ms + EPS) * g) * (1.0 + sc) + sh


def _ffn_kernel(x_ref, sh_ref, sc_ref, gt_ref, g_ref, win_ref, wout_ref, o_ref):
    x = x_ref[...]
    ub = _modulated_norm(x, g_ref[...], sc_ref[...], sh_ref[...]).astype(BF16)
    acc = None
    for j in range(D_FF // FF_CHUNK):
        cols = slice(j * FF_CHUNK, (j + 1) * FF_CHUNK)
        a = jnp.dot(ub, win_ref[:, cols], preferred_element_type=F32)
        b = jnp.dot(ub, win_ref[:, D_FF + j * FF_CHUNK:D_FF + (j + 1) * FF_CHUNK], preferred_element_type=F32)
        h = (a * _sigmoid(a) * b).astype(BF16)
        d = jnp.dot(h, wout_ref[cols, :], preferred_element_type=F32)
        acc = d if acc is None else acc + d
    o_ref[...] = x + (0.5 * gt_ref[...]) * acc


def _ffn(x, mod4, mod_idx, g, win3, wout3, tm=512):
    bsz, seq, d = x.shape
    nt = seq // tm
    row = lambda k: pl.BlockSpec((None, None, 1, d), lambda b, i: (b, k, 0, 0))
    return pl.pallas_call(
        _ffn_kernel,
        grid=(bsz, nt),
        in_specs=[pl.BlockSpec((None, tm, d), lambda b, i: (b, i, 0)),
                  row(mod_idx), row(mod_idx + 1), row(mod_idx + 2),
                  _resident((1, d)), _resident(win3.shape), _resident(wout3.shape)],
        out_specs=pl.BlockSpec((None, tm, d), lambda b, i: (b, i, 0)),
        out_shape=jax.ShapeDtypeStruct(x.shape, F32),
        compiler_params=_params(("arbitrary", "arbitrary")),
        name="ffn",
    )(x, mod4, mod4, mod4, g.reshape(1, d), win3, wout3)


def _mix_kernel(x_ref, sh_ref, sc_ref, g_ref, wqk_ref, wv_ref, wo_ref, wfox_ref, wg_ref, wgate_ref,
                bqk_ref, bv_ref, bo_ref, bfox_ref, bg_ref, bgate_ref, cw_ref, cb_ref,
                q_out, kt_out, v_out, o_out, fox_out, g_out, gate_out, halo_ref):
    i = pl.program_id(1)
    tm = x_ref.shape[0]

    @pl.when(i == 0)
    def _():
        halo_ref[...] = jnp.zeros_like(halo_ref)

    ub = _modulated_norm(x_ref[...], g_ref[...], sc_ref[...], sh_ref[...]).astype(BF16)

    def proj(w_ref, b_ref):
        return jnp.dot(ub, w_ref[...], preferred_element_type=F32) + b_ref[...]

    z = proj(wqk_ref, bqk_ref)
    gate_out[...] = proj(wgate_ref, bgate_ref)
    v_out[...] = proj(wv_ref, bv_ref).astype(BF16)
    o_out[...] = proj(wo_ref, bo_ref).astype(BF16)
    fox_out[...] = proj(wfox_ref, bfox_ref).astype(BF16)
    g_out[...] = proj(wg_ref, bg_ref).astype(BF16)
    halo = halo_ref[...]
    rows = lax.broadcasted_iota(jnp.int32, (tm, 1), 0)
    y = z * cw_ref[CONV_WIDTH - 1:CONV_WIDTH, :] + cb_ref[...]
    for k in range(1, CONV_WIDTH):
        zk = pltpu.roll(z, k, 0)
        hk = pltpu.roll(halo, k, 0)
        hk_full = jnp.concatenate([hk, zk[SUBLANES:, :]], axis=0)
        zk = jnp.where(rows < k, hk_full, zk)
        y = y + zk * cw_ref[CONV_WIDTH - 1 - k:CONV_WIDTH - k, :]
    halo_ref[...] = z[tm - SUBLANES:, :]
    y = y * _sigmoid(y)
    lane = lax.broadcasted_iota(jnp.int32, (1, 2 * MLSTM_QK), 1)
    y = y * jnp.where(lane < MLSTM_QK, 1.0, MLSTM_DQK ** -0.5)
    q_out[...] = y[:, :MLSTM_QK].astype(BF16)
    kt_out[...] = y[:, MLSTM_QK:].T.astype(BF16)


def _mix(x, mod4, g, w, tm=256):
    bsz, seq, d = x.shape
    nt = seq // tm
    row = lambda k: pl.BlockSpec((None, None, 1, d), lambda b, i: (b, k, 0, 0))
    tile = lambda n: pl.BlockSpec((None, tm, n), lambda b, i: (b, i, 0))
    names = ("wqk", "wv", "wo", "wfox", "wg", "wgate", "bqk", "bv", "bo", "bfox", "bg", "bgate", "cw", "cb")
    widths = (MLSTM_V, MLSTM_V, 3 * FOX_W, 2 * D_MODEL, LANES)
    dtypes = (BF16, BF16, BF16, BF16, F32)
    return pl.pallas_call(
        _mix_kernel,
        grid=(bsz, nt),
        in_specs=[tile(d), row(3), row(4), _resident((1, d))] + [_resident(w[n].shape) for n in names],
        out_specs=[tile(MLSTM_QK), pl.BlockSpec((None, MLSTM_QK, tm), lambda b, i: (b, 0, i))]
        + [tile(n) for n in widths],
        out_shape=[jax.ShapeDtypeStruct((bsz, seq, MLSTM_QK), BF16), jax.ShapeDtypeStruct((bsz, MLSTM_QK, seq), BF16)]
        + [jax.ShapeDtypeStruct((bsz, seq, n), dt) for n, dt in zip(widths, dtypes)],
        scratch_shapes=[pltpu.VMEM((SUBLANES, 2 * MLSTM_QK), F32)],
        compiler_params=_params(("arbitrary", "arbitrary")),
        name="mix",
    )(x, mod4, mod4, g.reshape(1, d), *[w[n] for n in names])


def _split3(x):
    hi = x.astype(BF16).astype(F32)
    r = x - hi
    mid = r.astype(BF16).astype(F32)
    lo = (r - mid).astype(BF16).astype(F32)
    return hi, mid, lo


def _gates_kernel(z_ref, g_ref, kb_ref, mtb_ref, bb_ref, sc_ref, carry_ref):
    ci = pl.program_id(1)
    L = MLSTM_CHUNK
    nch = z_ref.shape[0] // L

    @pl.when(ci == 0)
    def _():
        carry_ref[...] = jnp.zeros_like(carry_ref)

    lane = lax.broadcasted_iota(jnp.int32, (1, LANES), 1)
    row = lax.broadcasted_iota(jnp.int32, (L, L), 0)
    col = lax.broadcasted_iota(jnp.int32, (L, L), 1)
    tril = (col <= row).astype(BF16)
    rows = lax.broadcasted_iota(jnp.int32, (L, 1), 0)
    nrep = mtb_ref.shape[1]
    r = lax.broadcasted_iota(jnp.int32, (LANES, nrep), 0)
    c = lax.broadcasted_iota(jnp.int32, (LANES, nrep), 1)
    sel = jnp.where(r == MLSTM_HEADS + (c >> 7), 1.0, 0.0).astype(BF16)

    def replicate(x, nparts):
        return sum(jnp.dot(p.astype(BF16), sel, preferred_element_type=F32) for p in _split3(x)[:nparts])

    nb = kb_ref.shape[1]
    r = lax.broadcasted_iota(jnp.int32, (LANES, nb), 0)
    c = lax.broadcasted_iota(jnp.int32, (LANES, nb), 1)
    src = 2 * MLSTM_HEADS + 2 * (c >> 7)
    w = c & (LANES - 1)
    selb = jnp.concatenate(
        [jnp.where(((w == 3 + t) & (r == src)) | ((w == 6 + t) & (r == src + 1)), -1.0, 0.0).astype(BF16)
         for t in range(3)], axis=0)
    kb_ones = jnp.where(w[0:1, :] < 3, 1.0, 0.0)

    fcarry = carry_ref[0:1, :]
    m = carry_ref[1:2, :]
    for k in range(nch):
        sl = slice(k * L, (k + 1) * L)
        z = z_ref[sl, :]
        lf = jnp.minimum(z, 0.0) - jnp.log1p(jnp.exp(-jnp.abs(z)))
        lf3 = jnp.concatenate([p.astype(BF16) for p in _split3(lf)], axis=1)
        c3 = jnp.dot(tril, lf3, preferred_element_type=F32)
        cum = c3[:, :LANES] + c3[:, LANES:2 * LANES] + c3[:, 2 * LANES:]
        fcum = cum + fcarry
        fcarry = fcum[L - 1:L, :]
        a = pltpu.roll(z, MLSTM_HEADS, 1) - cum
        cmax = a
        s = 1
        while s < L:
            cmax = jnp.where(rows >= s, jnp.maximum(cmax, pltpu.roll(cmax, s, 0)), cmax)
            s *= 2
        mt = jnp.maximum(m, cmax)
        b_last = cum[L - 1:L, :]
        m_new = b_last + jnp.maximum(m, cmax[L - 1:L, :])
        g_ref[sl, :] = jnp.where(lane < MLSTM_HEADS, z, jnp.where(lane < 2 * MLSTM_HEADS, a, fcum))
        sc_ref[k] = jnp.concatenate([m, b_last + m - m_new, b_last - m_new,
                                     jnp.zeros((SUBLANES - 3, LANES), F32)], axis=0)
        m = m_new
        mtb_ref[sl, :] = replicate(mt, 1)
        bb_ref[sl, :] = replicate(cum, 2)
        parts3 = jnp.concatenate([p.astype(BF16) for p in _split3(fcum * LOG2E)], axis=1)
        kb_ref[sl, :] = (kb_ones + jnp.dot(parts3, selb, preferred_element_type=F32)).astype(BF16)
    carry_ref[0:1, :] = fcarry
    carry_ref[1:2, :] = m


def _gates(z):
    bsz, seq, n = z.shape
    tm = seq
    nb = (FOX_HEADS // 2) * LANES
    nrep = MLSTM_HEADS * LANES
    tile = lambda w: pl.BlockSpec((None, tm, w), lambda b, i: (b, i, 0))
    return pl.pallas_call(
        _gates_kernel,
        grid=(bsz, seq // tm),
        in_specs=[tile(n)],
        out_specs=[tile(n), tile(nb), tile(nrep), tile(nrep),
                   pl.BlockSpec((None, tm // MLSTM_CHUNK, SUBLANES, LANES), lambda b, i: (b, i, 0, 0))],
        out_shape=[jax.ShapeDtypeStruct(z.shape, F32), jax.ShapeDtypeStruct((bsz, seq, nb), BF16),
                   jax.ShapeDtypeStruct((bsz, seq, nrep), F32), jax.ShapeDtypeStruct((bsz, seq, nrep), F32),
                   jax.ShapeDtypeStruct((bsz, seq // MLSTM_CHUNK, SUBLANES, LANES), F32)],
        scratch_shapes=[pltpu.VMEM((SUBLANES, n), F32)],
        compiler_params=_params(("arbitrary", "arbitrary")),
        name="gates",
    )(z)


def _mlstm_kernel(q_ref, kt_ref, v_ref, o_ref, mtb_ref, bb_ref, arow_ref, sc_ref, ng_ref, y_ref, c_ref, n_ref):
    ci = pl.program_id(1)
    L = q_ref.shape[0]

    @pl.when(ci == 0)
    def _():
        c_ref[...] = jnp.zeros_like(c_ref)
        n_ref[...] = jnp.zeros_like(n_ref)

    row = lax.broadcasted_iota(jnp.int32, (L, L), 0)
    col = lax.broadcasted_iota(jnp.int32, (L, L), 1)
    causal = col <= row
    ones_l = jnp.ones((L, LANES), BF16)
    ones_v = jnp.ones((MLSTM_DV, LANES), BF16)
    arow = arow_ref[...]
    sc = sc_ref[...]
    for h in range(MLSTM_HEADS):
        q = q_ref[:, h * MLSTM_DQK:(h + 1) * MLSTM_DQK]
        kt = kt_ref[h * MLSTM_DQK:(h + 1) * MLSTM_DQK, :]
        v = v_ref[:, h * MLSTM_DV:(h + 1) * MLSTM_DV]
        mtb = mtb_ref[:, h * LANES:(h + 1) * LANES]
        bb = bb_ref[:, h * LANES:(h + 1) * LANES]
        a_r = arow[h:h + 1, :]
        lh = MLSTM_HEADS + h
        m, dec, wko = sc[0:1, lh:lh + 1], sc[1:2, lh:lh + 1], sc[2:3, lh:lh + 1]
        cmat = c_ref[h]
        nmat = n_ref[h]

        mt2 = jnp.concatenate([mtb] * (L // LANES), axis=1)
        p = jnp.exp(jnp.where(causal, a_r - mt2, -jnp.inf))
        s = (jnp.dot(q, kt, preferred_element_type=F32) * p).astype(BF16)
        w_inter = jnp.exp(m - mtb)
        qw = (q.astype(F32) * w_inter).astype(BF16)
        num = (jnp.dot(s, v, preferred_element_type=F32)
               + jnp.dot(qw, cmat.astype(BF16), preferred_element_type=F32))
        den = (w_inter * jnp.dot(q, nmat.astype(BF16), preferred_element_type=F32)
               + jnp.dot(s, ones_l, preferred_element_type=F32))
        dmax = jnp.maximum(jnp.abs(den), jnp.exp(-bb - mtb))
        msn = jnp.dot((num * num).astype(BF16), ones_v, preferred_element_type=F32) * (1.0 / MLSTM_DV)
        r = lax.rsqrt(msn + EPS * dmax * dmax)
        r2 = jnp.concatenate([r] * (MLSTM_DV // LANES), axis=1)
        og = o_ref[:, h * MLSTM_DV:(h + 1) * MLSTM_DV].astype(F32)
        y_ref[:, h * MLSTM_DV:(h + 1) * MLSTM_DV] = (_sigmoid(og) * (num * r2 * ng_ref[h:h + 1, :])).astype(BF16)

        kwt = (kt.astype(F32) * jnp.exp(a_r + wko)).astype(BF16)
        decay = jnp.exp(dec)
        c_ref[h] = decay * cmat + jnp.dot(kwt, v, preferred_element_type=F32)
        n_ref[h] = decay * nmat + jnp.dot(kwt, ones_l, preferred_element_type=F32)


def _mlstm(q, kt, v, o, mtb, bb, arow, sc, ng):
    bsz, seq, _ = q.shape
    L = MLSTM_CHUNK
    nc = seq // L
    tile = lambda n: pl.BlockSpec((None, L, n), lambda b, c: (b, c, 0))
    return pl.pallas_call(
        _mlstm_kernel,
        grid=(bsz, nc),
        in_specs=[tile(MLSTM_QK), pl.BlockSpec((None, MLSTM_QK, L), lambda b, c: (b, 0, c)),
                  tile(MLSTM_V), tile(MLSTM_V), tile(MLSTM_HEADS * LANES), tile(MLSTM_HEADS * LANES),
                  pl.BlockSpec((None, None, MLSTM_HEADS, L), lambda b, c: (b, c, 0, 0)),
                  pl.BlockSpec((None, None, SUBLANES, LANES), lambda b, c: (b, c, 0, 0)),
                  _resident(ng.shape)],
        out_specs=tile(MLSTM_V),
        out_shape=jax.ShapeDtypeStruct((bsz, seq, MLSTM_V), BF16),
        scratch_shapes=[pltpu.VMEM((MLSTM_HEADS, MLSTM_DQK, MLSTM_DV), F32),
                        pltpu.VMEM((MLSTM_HEADS, MLSTM_DQK, LANES), F32)],
        compiler_params=_params(("arbitrary", "arbitrary")),
        name="mlstm",
    )(q, kt, v, o, mtb, bb, arow, sc, ng)


def _fox_kernel(q_ref, k_ref, v_ref, kb_ref, fr_ref, qg_ref, kg_ref, y_ref,
                kaug_ref, vt_ref, qt_ref, acc_ref, m_ref, *, bq, bk):
    seq = k_ref.shape[0]
    ratio = bq // bk
    nq = seq // bq
    nk = seq // bk

    li = lax.broadcasted_iota(jnp.int32, (LANES, LANES), 0) // FOX_DH
    lj = lax.broadcasted_iota(jnp.int32, (LANES, LANES), 1) // FOX_DH
    same_head = (li == lj).astype(BF16)
    rowv = lax.broadcasted_iota(jnp.int32, (FOX_VROWS - FOX_DH, bk), 0)
    ones_rows = jnp.where(rowv == 0, 1.0, 0.0)
    for jb in range(nk):
        sl = slice(jb * bk, (jb + 1) * bk)
        kf = k_ref[sl, :].astype(F32)
        ss = jnp.dot((kf * kf).astype(BF16), same_head, preferred_element_type=F32)
        kaug_ref[sl, :LANES] = (kf * lax.rsqrt(ss * (1.0 / FOX_DH) + EPS) * kg_ref[...]).astype(BF16)
        kaug_ref[sl, LANES:] = kb_ref[sl, :]
        vtt = v_ref[sl, :].astype(F32).T
        for h in range(2):
            vt_ref[jb, h] = jnp.concatenate([vtt[h * FOX_DH:(h + 1) * FOX_DH], ones_rows], axis=0).astype(BF16)

    rowi = lax.broadcasted_iota(jnp.int32, (LANES, 1), 0)
    top = rowi < FOX_DH
    for qi in range(nq):
        sl = slice(qi * bq, (qi + 1) * bq)
        qt = q_ref[sl, :].astype(F32).T
        sq = qt * qt
        inv0 = lax.rsqrt(jnp.sum(sq[:FOX_DH], axis=0, keepdims=True) * (1.0 / FOX_DH) + EPS)
        inv1 = lax.rsqrt(jnp.sum(sq[FOX_DH:], axis=0, keepdims=True) * (1.0 / FOX_DH) + EPS)
        qn = qt * jnp.where(top, inv0, inv1) * (qg_ref[...] * (FOX_DH ** -0.5 * LOG2E))
        fr = fr_ref[:, sl] * LOG2E
        for h in range(2):
            hi, mid, lo = _split3(fr[h:h + 1, :])
            ones = (rowi >= 3 + 3 * h) & (rowi < 6 + 3 * h)
            qaug = jnp.where(rowi == 0, hi, jnp.where(rowi == 1, mid, jnp.where(rowi == 2, lo,
                                                                                jnp.where(ones, 1.0, 0.0))))
            qt_ref[qi, h, :LANES, :] = jnp.where(top == (h == 0), qn, 0.0).astype(BF16)
            qt_ref[qi, h, LANES:, :] = qaug.astype(BF16)

    def scores(qi, j, lo):
        kb = kaug_ref[j * bk:(j + 1) * bk, :]
        return [jnp.dot(kb, qt_ref[qi, h, :, lo:] if lo else qt_ref[qi, h], preferred_element_type=F32)
                for h in range(2)]

    def softmax_step(j, lo, mask_off, sts):
        w = bq - lo
        if mask_off is not None:
            row = lax.broadcasted_iota(jnp.int32, (bk, w), 0)
            col = lax.broadcasted_iota(jnp.int32, (bk, w), 1)
            keep = row + (mask_off - lo) <= col
        for h in range(2):
            st = sts[h]
            if mask_off is not None:
                st = jnp.where(keep, st, -jnp.inf)
            m_full = m_ref[h]
            m_prev = m_full[:, lo:] if lo else m_full
            m_new = jnp.maximum(m_prev, jnp.max(st, axis=0, keepdims=True))
            alpha = jnp.exp2(m_prev - m_new)
            pt = jnp.exp2(st - m_new).astype(BF16)
            if lo:
                acc_ref[h, :, lo:] = alpha * acc_ref[h, :, lo:] + jnp.dot(vt_ref[j, h], pt,
                                                                          preferred_element_type=F32)
                m_ref[h] = jnp.concatenate([m_full[:, :lo], m_new], axis=1)
            else:
                acc_ref[h] = alpha * acc_ref[h] + jnp.dot(vt_ref[j, h], pt, preferred_element_type=F32)
                m_ref[h] = m_new

    blocks = []
    for qi in range(nq):
        for j in range(ratio * qi + ratio):
            d = j - ratio * qi
            blocks.append((qi, j, 0 if d <= 0 else d * bk, None if d < 0 else d * bk))
    cur = scores(*blocks[0][:3])
    for i, (qi, j, lo, mask_off) in enumerate(blocks):
        if j == 0:
            m_ref[...] = jnp.full_like(m_ref, -jnp.inf)
            acc_ref[...] = jnp.zeros_like(acc_ref)
        nxt = scores(*blocks[i + 1][:3]) if i + 1 < len(blocks) else None
        softmax_step(j, lo, mask_off, cur)
        cur = nxt
        if j == ratio * qi + ratio - 1:
            ot = jnp.concatenate([acc_ref[h, :FOX_DH] / acc_ref[h, FOX_DH:FOX_DH + 1] for h in range(2)], axis=0)
            y_ref[qi * bq:(qi + 1) * bq, :] = ot.T.astype(BF16)


def _fox(fox, kbias, frow, qg, kg, bq=512, bk=256):
    bsz, seq, _ = fox.shape
    npair = FOX_HEADS // 2
    col = lambda off: pl.BlockSpec((None, seq, LANES), lambda b, p: (b, 0, off + p))
    return pl.pallas_call(
        functools.partial(_fox_kernel, bq=bq, bk=bk),
        grid=(bsz, npair),
        in_specs=[col(0), col(npair), col(2 * npair), col(0),
                  pl.BlockSpec((None, None, 2, seq), lambda b, p: (b, p, 0, 0)),
                  pl.BlockSpec((None, LANES, 1), lambda b, p: (p, 0, 0)),
                  pl.BlockSpec((None, 1, LANES), lambda b, p: (p, 0, 0))],
        out_specs=col(0),
        out_shape=jax.ShapeDtypeStruct((bsz, seq, FOX_W), BF16),
        scratch_shapes=[pltpu.VMEM((seq, 2 * LANES), BF16),
                        pltpu.VMEM((seq // bk, 2, FOX_VROWS, bk), BF16),
                        pltpu.VMEM((seq // bq, 2, 2 * LANES, bq), BF16),
                        pltpu.VMEM((2, FOX_VROWS, bq), F32),
                        pltpu.VMEM((2, 1, bq), F32)],
        compiler_params=_params(("arbitrary", "arbitrary")),
        name="fox",
    )(fox, fox, fox, kbias, frow, qg, kg)


def _merge_kernel(x_ref, ya_ref, yb_ref, ga_ref, gb_ref, gt_ref, wa_ref, wb_ref, wo_ref, o_ref):
    ma = jnp.dot(ya_ref[...], wa_ref[...], preferred_element_type=F32)
    mb = jnp.dot(yb_ref[...], wb_ref[...], preferred_element_type=F32)
    merged = _sigmoid(ga_ref[...].astype(F32)) * ma + _sigmoid(gb_ref[...].astype(F32)) * mb
    o_ref[...] = x_ref[...] + gt_ref[...] * jnp.dot(merged.astype(BF16), wo_ref[...], preferred_element_type=F32)


def _merge(x, ya, yb, g, mod4, wa, wb, wo, tm=512):
    bsz, seq, d = x.shape
    tile = lambda col: pl.BlockSpec((None, tm, d), lambda b, i: (b, i, col))
    return pl.pallas_call(
        _merge_kernel,
        grid=(bsz, seq // tm),
        in_specs=[tile(0), tile(0), tile(0), tile(0), tile(1),
                  pl.BlockSpec((None, None, 1, d), lambda b, i: (b, 5, 0, 0)),
                  _resident(wa.shape), _resident(wb.shape), _resident(wo.shape)],
        out_specs=tile(0),
        out_shape=jax.ShapeDtypeStruct(x.shape, F32),
        compiler_params=_params(("arbitrary", "arbitrary")),
        name="merge",
    )(x, ya, yb, g, g, mod4, wa, wb, wo)


def _ffn_weights(w_in, w_out):
    return w_in.astype(BF16), w_out.astype(BF16)


def _mix_weights(w_mix, b_mix, conv_w, conv_b):
    offs = _mix_offsets()
    seg = lambda a, i, j: a[..., offs[i]:offs[j]]
    wgate = jnp.concatenate([seg(w_mix, 4, 6), seg(w_mix, 9, 10)], axis=-1)
    wgate = jnp.pad(wgate, ((0, 0), (0, LANES - N_GATES))).astype(BF16)
    bgate = jnp.pad(jnp.concatenate([seg(b_mix, 4, 6), seg(b_mix, 9, 10)], axis=-1), (0, LANES - N_GATES))
    r = lambda a: a.reshape(1, -1)
    return {
        "wqk": seg(w_mix, 0, 2).astype(BF16), "wv": seg(w_mix, 2, 3).astype(BF16),
        "wo": seg(w_mix, 3, 4).astype(BF16), "wfox": seg(w_mix, 6, 9).astype(BF16),
        "wg": seg(w_mix, 10, 12).astype(BF16), "wgate": wgate,
        "bqk": r(seg(b_mix, 0, 2)), "bv": r(seg(b_mix, 2, 3)), "bo": r(seg(b_mix, 3, 4)),
        "bfox": r(seg(b_mix, 6, 9)), "bg": r(seg(b_mix, 10, 12)), "bgate": r(bgate),
        "cw": conv_w, "cb": r(conv_b),
    }


def _layer(x, c, w_ada, b_ada, ffn1_norm_g, ffn1_w_in, ffn1_w_out, mix_norm_g, w_mix, b_mix, conv_w, conv_b,
           mlstm_norm_g, fox_q_norm_g, fox_k_norm_g, w_branch_a, w_branch_b, w_out, ffn2_norm_g, ffn2_w_in,
           ffn2_w_out):
    bsz, seq, d = x.shape
    mod4 = _adaln(c, w_ada, b_ada).reshape(bsz, N_MOD, 1, d)

    x = _ffn(x, mod4, 0, ffn1_norm_g, *_ffn_weights(ffn1_w_in, ffn1_w_out))

    q, kt, v, o, fox, g, gate = _mix(x, mod4, mix_norm_g, _mix_weights(w_mix, b_mix, conv_w, conv_b))
    gcol, kbias, mtb, bb, sc = _gates(gate)
    nc = seq // MLSTM_CHUNK
    arow = gcol[:, :, MLSTM_HEADS:2 * MLSTM_HEADS].reshape(bsz, nc, MLSTM_CHUNK, MLSTM_HEADS).transpose(0, 1, 3, 2)
    fcum = gcol[:, :, 2 * MLSTM_HEADS:N_GATES].reshape(bsz, seq, FOX_HEADS // 2, 2)
    frow = fcum.transpose(0, 2, 3, 1)

    ya = _mlstm(q, kt, v, o, mtb, bb, arow, sc, mlstm_norm_g)
    yb = _fox(fox, kbias, frow, fox_q_norm_g.reshape(FOX_HEADS // 2, LANES, 1),
              fox_k_norm_g.reshape(FOX_HEADS // 2, 1, LANES))
    x = _merge(x, ya, yb, g, mod4, w_branch_a.astype(BF16), w_branch_b.astype(BF16), w_out.astype(BF16))

    return _ffn(x, mod4, 6, ffn2_norm_g, *_ffn_weights(ffn2_w_in, ffn2_w_out))


def kernel(x, c, w_ada, b_ada, ffn1_norm_g, ffn1_w_in, ffn1_w_out, mix_norm_g, w_mix, b_mix, conv_w, conv_b,
           mlstm_norm_g, fox_q_norm_g, fox_k_norm_g, w_branch_a, w_branch_b, w_out, ffn2_norm_g, ffn2_w_in,
           ffn2_w_out):
    for l in range(w_ada.shape[0]):
        x = _layer(x, c, w_ada[l], b_ada[l], ffn1_norm_g[l], ffn1_w_in[l], ffn1_w_out[l], mix_norm_g[l], w_mix[l],
                   b_mix[l], conv_w[l], conv_b[l], mlstm_norm_g[l], fox_q_norm_g[l], fox_k_norm_g[l],
                   w_branch_a[l], w_branch_b[l], w_out[l], ffn2_norm_g[l], ffn2_w_in[l], ffn2_w_out[l])
    return x
```
